```python
import math
import jax, jax.numpy as jnp
from jax import lax
import numpy as np

D_MODEL = 2048
BATCH = 2
SEQ = 8192
DEPTH = 4

GRID_W = 64
CTX_LEN = 256
NA_HEADS = 8
NA_HEAD_DIM = 128
NA_WIN_H = 8
NA_WIN_W = 16
ROPE_THETA = 10000.0
ML_HEADS = 4
ML_QK_DIM = 128
ML_V_DIM = 256
ML_CHUNK = 64
ML_CONV = 3
D_FF = 5632
FFN_CONV = 3
EPS = 1e-6
M_INIT = -1e30

NA_W = NA_HEADS * NA_HEAD_DIM
ML_QK_W = ML_HEADS * ML_QK_DIM
ML_V_W = ML_HEADS * ML_V_DIM
N_GATE = 2 * ML_HEADS
IN_WIDTHS = (NA_W, NA_W, NA_W, 2 * ML_QK_W, ML_V_W, ML_V_W, N_GATE, N_GATE, D_MODEL, D_MODEL)
IN_COLS = 3 * NA_W + 2 * ML_QK_W + 2 * ML_V_W + 2 * N_GATE + 2 * D_MODEL

kernel_name = "hybrid_na_mlstm_dit_trunk"


def rmsnorm(x, g):
    xf = x.astype(jnp.float32)
    y = xf * lax.rsqrt(jnp.mean(xf * xf, axis=-1, keepdims=True) + EPS)
    return (y * g.astype(jnp.float32)).astype(x.dtype)


def modulate(h, shift, scale):
    return h * (1 + scale) + shift


def dwconv(x, w, b):
    k = w.shape[0]
    y = lax.conv_general_dilated(
        x, w[:, None, :].astype(x.dtype), window_strides=(1,),
        padding=[(k // 2, k - 1 - k // 2)],
        dimension_numbers=("NWC", "WIO", "NWC"),
        feature_group_count=x.shape[-1])
    return y + b


def axial_rope_tables(n_tokens):
    t = jnp.arange(n_tokens)
    row = (t // GRID_W).astype(jnp.float32)
    col = (t % GRID_W).astype(jnp.float32)
    n_freq = NA_HEAD_DIM // 4
    inv_freq = ROPE_THETA ** (-jnp.arange(n_freq, dtype=jnp.float32) / n_freq)
    ang = jnp.concatenate([row[:, None] * inv_freq, col[:, None] * inv_freq], axis=-1)
    return jnp.cos(ang), jnp.sin(ang)


def apply_rope(x, cos, sin):
    x1, x2 = jnp.split(x.astype(jnp.float32), 2, axis=-1)
    cos = cos[None, :, None]
    sin = sin[None, :, None]
    return jnp.concatenate([x1 * cos - x2 * sin, x2 * cos + x1 * sin], axis=-1).astype(x.dtype)


def project_inputs(h, w_in, conv_w, conv_b, igate_b, fgate_b):
    B, S, _ = h.shape
    z = h @ w_in
    parts = []
    start = 0
    for width in IN_WIDTHS:
        parts.append(z[..., start:start + width])
        start += width
    na_q, na_k, na_v, ml_qk, ml_v, ml_o, ig, fg, g_na, g_ml = parts
    ml_qk = jax.nn.silu(dwconv(ml_qk, conv_w, conv_b))
    ml_q, ml_k = jnp.split(ml_qk, 2, axis=-1)
    log_i = ig.reshape(B, S, 2, ML_HEADS).astype(jnp.float32) + igate_b.astype(jnp.float32)
    log_f = jax.nn.log_sigmoid(fg.reshape(B, S, 2, ML_HEADS).astype(jnp.float32) + fgate_b.astype(jnp.float32))
    return (na_q.reshape(B, S, NA_HEADS, NA_HEAD_DIM),
            na_k.reshape(B, S, NA_HEADS, NA_HEAD_DIM),
            na_v.reshape(B, S, NA_HEADS, NA_HEAD_DIM),
            ml_q.reshape(B, S, ML_HEADS, ML_QK_DIM),
            ml_k.reshape(B, S, ML_HEADS, ML_QK_DIM),
            ml_v.reshape(B, S, ML_HEADS, ML_V_DIM),
            jax.nn.sigmoid(ml_o), log_i, log_f,
            jax.nn.sigmoid(g_na), jax.nn.sigmoid(g_ml))


def na_latent(q, k, v, kc, vc, rpb):
    B, S, H, Dh = q.shape
    rows = S // GRID_W
    kh = min(NA_WIN_H, rows)
    kw = NA_WIN_W
    r = jnp.arange(rows)
    col = jnp.arange(GRID_W)
    row_start = jnp.clip(r - kh // 2, 0, rows - kh)
    key_rows = row_start[:, None] + jnp.arange(kh)[None, :]
    col_start = jnp.clip(col - kw // 2, 0, GRID_W - kw)
    col_mask = (col[None, :] >= col_start[:, None]) & (col[None, :] < col_start[:, None] + kw)
    dr = key_rows - r[:, None] + NA_WIN_H - 1
    dc = jnp.clip(col[None, :] - col[:, None], 1 - kw, kw - 1) + kw - 1
    bias = rpb[:, dr[:, None, :, None], dc[None, :, None, :]].astype(jnp.float32)
    qg = q.reshape(B, rows, GRID_W, H, Dh)
    kg = k.reshape(B, rows, GRID_W, H, Dh)[:, key_rows]
    vg = v.reshape(B, rows, GRID_W, H, Dh)[:, key_rows]
    scale = Dh ** -0.5
    s_loc = jnp.einsum("brqhd,brikhd->bhrqik", qg, kg).astype(jnp.float32) * scale + bias[None]
    s_loc = jnp.where(col_mask[:, None, :], s_loc, -jnp.inf)
    s_ctx = jnp.einsum("brqhd,blhd->bhrql", qg, kc).astype(jnp.float32) * scale
    n_loc = kh * GRID_W
    p = jax.nn.softmax(jnp.concatenate([s_loc.reshape(B, H, rows, GRID_W, n_loc), s_ctx], axis=-1), axis=-1)
    p = p.astype(v.dtype)
    p_loc = p[..., :n_loc].reshape(B, H, rows, GRID_W, kh, GRID_W)
    p_ctx = p[..., n_loc:]
    o = jnp.einsum("bhrqik,brikhd->brqhd", p_loc, vg) + jnp.einsum("bhrql,blhd->brqhd", p_ctx, vc)
    return o.reshape(B, S, H * Dh)


def na_context(qc, kc, vc):
    B, L, H, Dh = qc.shape
    s = jnp.einsum("blhd,bmhd->bhlm", qc, kc).astype(jnp.float32) * Dh ** -0.5
    p = jax.nn.softmax(s, axis=-1).astype(vc.dtype)
    return jnp.einsum("bhlm,bmhd->blhd", p, vc).reshape(B, L, H * Dh)


def mlstm_direction(q, k, v, log_i, log_f, state0, with_output):
    B, S, H, dk = k.shape
    dv = v.shape[-1]
    L = ML_CHUNK
    N = S // L
    qc = q.astype(jnp.float32).reshape(B, N, L, H, dk) * dk ** -0.5
    kc = k.astype(jnp.float32).reshape(B, N, L, H, dk)
    vc = v.astype(jnp.float32).reshape(B, N, L, H, dv)
    li = log_i.reshape(B, N, L, H)
    b = jnp.cumsum(log_f.reshape(B, N, L, H), axis=2)
    g = b[:, :, -1]
    a = g[:, :, None] - b + li
    ma = jnp.max(a, axis=2)
    w = jnp.exp(a - ma[:, :, None])
    c_loc = jnp.einsum("bnlh,bnlhk,bnlhv->nbhkv", w, kc, vc)
    n_loc = jnp.einsum("bnlh,bnlhk->nbhk", w, kc)

    def step(carry, inp):
        C, n, m = carry
        C_l, n_l, m_l, g_l = inp
        m_new = jnp.maximum(g_l + m, m_l)
        dec = jnp.exp(g_l + m - m_new)
        inc = jnp.exp(m_l - m_new)
        C_new = dec[..., None, None] * C + inc[..., None, None] * C_l
        n_new = dec[..., None] * n + inc[..., None] * n_l
        return (C_new, n_new, m_new), (C, n, m)

    final, entering = lax.scan(step, state0, (c_loc, n_loc, jnp.moveaxis(ma, 1, 0), jnp.moveaxis(g, 1, 0)))
    if not with_output:
        return None, final
    c_p, n_p, m_p = entering
    inter = b + jnp.moveaxis(m_p, 0, 1)[:, :, None]
    tri = jnp.tril(jnp.ones((L, L), dtype=bool))
    d = b[:, :, :, None] - b[:, :, None] + li[:, :, None]
    d = jnp.where(tri[:, :, None], d, -jnp.inf)
    m_t = jnp.maximum(inter, jnp.max(d, axis=3))
    w_inter = jnp.exp(inter - m_t)
    w_intra = jnp.exp(d - m_t[:, :, :, None])
    s = jnp.einsum("bnthk,bnshk->bntsh", qc, kc) * w_intra
    num = jnp.einsum("bntsh,bnshv->bnthv", s, vc) + w_inter[..., None] * jnp.einsum("bnthk,nbhkv->bnthv", qc, c_p)
    den = jnp.sum(s, axis=3) + w_inter * jnp.einsum("bnthk,nbhk->bnth", qc, n_p)
    den = jnp.maximum(jnp.abs(den), jnp.exp(-m_t))
    h = (num / den[..., None]).reshape(B, S, H, dv)
    return h, final


def orient(t, direction):
    return t if direction == 0 else jnp.flip(t, axis=1)


def merge_branches(na_o, ml_h, ml_o, g_na, g_ml, norm_g, w_na_proj, w_ml_proj, w_out):
    B, S, H, dv = ml_h.shape
    ml = ml_h * lax.rsqrt(jnp.mean(ml_h * ml_h, axis=-1, keepdims=True) + EPS)
    ml = (ml.reshape(B, S, H * dv) * norm_g.astype(jnp.float32)).astype(ml_o.dtype) * ml_o
    y = g_na * (na_o @ w_na_proj) + g_ml * (ml @ w_ml_proj)
    return y @ w_out


def token_mixer(hx, hc, w_in, rpb, conv_w, conv_b, igate_b, fgate_b, norm_g, w_na_proj, w_ml_proj, w_out,
                cos, sin, with_ctx_out):
    qx, kx, vx, mqx, mkx, mvx, ox, lix, lfx, gnx, gmx = project_inputs(hx, w_in, conv_w, conv_b, igate_b, fgate_b)
    qc, kc, vc, mqc, mkc, mvc, oc, lic, lfc, gnc, gmc = project_inputs(hc, w_in, conv_w, conv_b, igate_b, fgate_b)
    na_x = na_latent(apply_rope(qx, cos, sin), apply_rope(kx, cos, sin), vx, kc, vc, rpb)
    B = hc.shape[0]
    state0 = (jnp.zeros((B, ML_HEADS, ML_QK_DIM, ML_V_DIM), jnp.float32),
              jnp.zeros((B, ML_HEADS, ML_QK_DIM), jnp.float32),
              jnp.full((B, ML_HEADS), M_INIT, jnp.float32))
    ml_x = 0.0
    ml_c = 0.0
    for direction in range(2):
        h_c, st = mlstm_direction(orient(mqc, direction), orient(mkc, direction), orient(mvc, direction),
                                  orient(lic[:, :, direction], direction), orient(lfc[:, :, direction], direction),
                                  state0, with_ctx_out)
        h_x, _ = mlstm_direction(orient(mqx, direction), orient(mkx, direction), orient(mvx, direction),
                                 orient(lix[:, :, direction], direction), orient(lfx[:, :, direction], direction),
                                 st, True)
        ml_x = ml_x + orient(h_x, direction)
        if with_ctx_out:
            ml_c = ml_c + orient(h_c, direction)
    y_x = merge_branches(na_x, ml_x, ox, gnx, gmx, norm_g, w_na_proj, w_ml_proj, w_out)
    if not with_ctx_out:
        return y_x, None
    na_c = na_context(qc, kc, vc)
    y_c = merge_branches(na_c, ml_c, oc, gnc, gmc, norm_g, w_na_proj, w_ml_proj, w_out)
    return y_x, y_c


def conv_ffn(h, w_up, conv_w, conv_b, w_down):
    gate, val = jnp.split(h @ w_up, 2, axis=-1)
    return (jax.nn.silu(dwconv(gate, conv_w, conv_b)) * val) @ w_down


def setup_inputs(seed: int = 0) -> dict:
    key = jax.random.key(seed)
    ks = jax.random.split(key, 26)
    f32 = jnp.float32

    def nrm(k, shape, scale):
        return jax.random.normal(k, shape, f32) * scale

    L = DEPTH
    D = D_MODEL
    f_bias = jnp.broadcast_to(jnp.linspace(3.0, 6.0, ML_HEADS, dtype=f32), (L, 2, ML_HEADS))
    return {
        "x": nrm(ks[0], (BATCH, SEQ, D), 1.0),
        "c": nrm(ks[1], (BATCH, D), 1.0),
        "ctx": nrm(ks[2], (BATCH, CTX_LEN, D), 1.0),
        "c_ctx": nrm(ks[3], (D,), 1.0),
        "w_mod": nrm(ks[4], (L, D, 6 * D), 0.5 * D ** -0.5),
        "b_mod": nrm(ks[5], (L, 6 * D), 0.01),
        "norm_mix_pre": 1.0 + nrm(ks[6], (L, D), 0.02),
        "norm_mix_post": 1.0 + nrm(ks[7], (L, D), 0.02),
        "norm_ffn_pre": 1.0 + nrm(ks[8], (L, D), 0.02),
        "norm_ffn_post": 1.0 + nrm(ks[9], (L, D), 0.02),
        "w_in": nrm(ks[10], (L, D, IN_COLS), D ** -0.5),
        "na_rpb": nrm(ks[11], (L, NA_HEADS, 2 * NA_WIN_H - 1, 2 * NA_WIN_W - 1), 0.1),
        "ml_conv_w": nrm(ks[12], (L, ML_CONV, 2 * ML_QK_W), ML_CONV ** -0.5),
        "ml_conv_b": nrm(ks[13], (L, 2 * ML_QK_W), 0.01),
        "ml_igate_b": nrm(ks[14], (L, 2, ML_HEADS), 0.1),
        "ml_fgate_b": f_bias + nrm(ks[15], (L, 2, ML_HEADS), 0.1),
        "ml_norm_g": 1.0 + nrm(ks[16], (L, ML_V_W), 0.02),
        "w_na_proj": nrm(ks[17], (L, NA_W, D), NA_W ** -0.5),
        "w_ml_proj": nrm(ks[18], (L, ML_V_W, D), ML_V_W ** -0.5),
        "w_out": nrm(ks[19], (L, D, D), D ** -0.5),
        "w_up": nrm(ks[20], (L, D, 2 * D_FF), D ** -0.5),
        "ffn_conv_w": nrm(ks[21], (L, FFN_CONV, D_FF), FFN_CONV ** -0.5),
        "ffn_conv_b": nrm(ks[22], (L, D_FF), 0.01),
        "w_down": nrm(ks[23], (L, D_FF, D), D_FF ** -0.5),
    }


def reference(x, c, ctx, c_ctx, w_mod, b_mod, norm_mix_pre, norm_mix_post, norm_ffn_pre, norm_ffn_post,
              w_in, na_rpb, ml_conv_w, ml_conv_b, ml_igate_b, ml_fgate_b, ml_norm_g,
              w_na_proj, w_ml_proj, w_out, w_up, ffn_conv_w, ffn_conv_b, w_down):
    cos, sin = axial_rope_tables(x.shape[1])
    silu_c = jax.nn.silu(c)
    silu_cc = jax.nn.silu(c_ctx)
    for l in range(DEPTH):
        last = l == DEPTH - 1
        mx = jnp.split((silu_c @ w_mod[l] + b_mod[l])[:, None, :], 6, axis=-1)
        mc = jnp.split(silu_cc @ w_mod[l] + b_mod[l], 6, axis=-1)
        hx = modulate(rmsnorm(x, norm_mix_pre[l]), mx[0], mx[1])
        hc = modulate(rmsnorm(ctx, norm_mix_pre[l]), mc[0], mc[1])
        yx, yc = token_mixer(hx, hc, w_in[l], na_rpb[l], ml_conv_w[l], ml_conv_b[l], ml_igate_b[l], ml_fgate_b[l],
                             ml_norm_g[l], w_na_proj[l], w_ml_proj[l], w_out[l], cos, sin, not last)
        x = x + mx[2] * rmsnorm(yx, norm_mix_post[l])
        hx = modulate(rmsnorm(x, norm_ffn_pre[l]), mx[3], mx[4])
        x = x + mx[5] * rmsnorm(conv_ffn(hx, w_up[l], ffn_conv_w[l], ffn_conv_b[l], w_down[l]), norm_ffn_post[l])
        if not last:
            ctx = ctx + mc[2] * rmsnorm(yc, norm_mix_post[l])
            hc = modulate(rmsnorm(ctx, norm_ffn_pre[l]), mc[3], mc[4])
            ctx = ctx + mc[5] * rmsnorm(conv_ffn(hc, w_up[l], ffn_conv_w[l], ffn_conv_b[l], w_down[l]), norm_ffn_post[l])
    return x
```

```python
import functools

import numpy as np
import jax
import jax.numpy as jnp
from jax import lax
from jax.experimental import pallas as pl
from jax.experimental.pallas import tpu as pltpu

GRID_W = 64
NA_HEAD_DIM = 128
NA_WIN_H = 8
NA_WIN_W = 16
ROPE_THETA = 10000.0
ML_QK_DIM = 128
ML_V_DIM = 256
EPS = 1e-6
M_INIT = -1e30
NEG_BIG = -1e30

LANES = 128
BF16_ROWS = 16
MOD_ROWS = 16
CHUNK = 256
NA_KEY_ROWS = 12
VMEM_LIMIT = 56 * 1024 * 1024

F32 = jnp.float32
BF16 = jnp.bfloat16


def _dot(a, b):
    return jnp.dot(a, b, preferred_element_type=F32)


def _dot_nt(a, b):
    return lax.dot_general(a, b, (((1,), (1,)), ((), ())), preferred_element_type=F32)


def _dot_tn(a, b):
    return lax.dot_general(a, b, (((0,), (0,)), ((), ())), preferred_element_type=F32)


def _sigmoid(x):
    return 1.0 / (1.0 + jnp.exp(-x))


def _log_sigmoid(x):
    return jnp.minimum(x, 0.0) - jnp.log(1.0 + jnp.exp(-jnp.abs(x)))


def _split3(x):
    hi = x.astype(BF16)
    r1 = x - hi.astype(F32)
    mid = r1.astype(BF16)
    lo = (r1 - mid.astype(F32)).astype(BF16)
    return hi, mid, lo


def _params(n_grid):
    return pltpu.CompilerParams(dimension_semantics=("arbitrary",) * n_grid,
                                vmem_limit_bytes=VMEM_LIMIT)


def _pick_tile(n, candidates):
    for c in candidates:
        if n % c == 0:
            return c
    raise ValueError(f"no tile for {n} among {candidates}")


def _mod_kernel(c_ref, w_ref, b_ref, o_ref):
    c = c_ref[...]
    s = c * _sigmoid(c)
    hi, mid, lo = _split3(s)
    whi, wmid, wlo = _split3(w_ref[0])
    acc = _dot(hi, whi) + (_dot(hi, wmid) + _dot(mid, whi)) + (_dot(hi, wlo) + _dot(mid, wmid) + _dot(lo, whi))
    o_ref[0] = acc + b_ref[0]


def _modulation(crow, w_mod, b_mod):
    L, D, N = w_mod.shape
    tn = _pick_tile(N, (1024, 512, 256, 128))
    return pl.pallas_call(
        _mod_kernel,
        grid=(L, N // tn),
        in_specs=[pl.BlockSpec((MOD_ROWS, D), lambda l, j: (0, 0)),
                  pl.BlockSpec((1, D, tn), lambda l, j: (l, 0, j)),
                  pl.BlockSpec((1, 1, tn), lambda l, j: (l, 0, j))],
        out_specs=pl.BlockSpec((1, MOD_ROWS, tn), lambda l, j: (l, 0, j)),
        out_shape=jax.ShapeDtypeStruct((L, MOD_ROWS, N), F32),
        compiler_params=_params(2),
        name="adaln_mod",
    )(crow, w_mod, b_mod.reshape(L, 1, N))


def _norm_mod(x, g, mod_ref, first_row, n_lat, which):
    rows = x.shape[0]
    var = jnp.mean(x * x, axis=-1, keepdims=True)
    y = x * lax.rsqrt(var + EPS) * g
    row = first_row + lax.broadcasted_iota(jnp.int32, (rows, 1), 0)
    is_ctx = row >= n_lat
    shift = jnp.where(is_ctx, mod_ref[0, 1, which:which + 1, :], mod_ref[0, 0, which:which + 1, :])
    scale = jnp.where(is_ctx, mod_ref[0, 1, which + 1:which + 2, :], mod_ref[0, 0, which + 1:which + 2, :])
    return y * (1.0 + scale) + shift


def _inproj_kernel(x_ref, mod_ref, g_ref, cs_ref, sn_ref, w_ref, wg_ref, wgt_ref,
                   z_ref, gcol_ref, grow_ref, h_scr, *, n_lat, tm, tn, na_w, n_rope, n_none, q_scale):
    i = pl.program_id(1)
    j = pl.program_id(2)

    @pl.when(j == 0)
    def _():
        h = _norm_mod(x_ref[0], g_ref[...], mod_ref, i * tm, n_lat, 0).astype(BF16)
        h_scr[...] = h
        gcol_ref[0] = _dot(h, wg_ref[...])
        grow_ref[0] = _dot_nt(wgt_ref[...], h)

    acc = _dot(h_scr[...], w_ref[...])

    for jj in range(n_rope):
        @pl.when(j == jj)
        def _(jj=jj):
            cs = cs_ref[...]
            sn = sn_ref[...]
            for c in range(tn // LANES):
                chunk = acc[:, c * LANES:(c + 1) * LANES]
                out = chunk * cs + pltpu.roll(chunk, LANES // 2, axis=1) * sn
                if jj * tn + c * LANES < na_w:
                    out = out * q_scale
                z_ref[0, :, c * LANES:(c + 1) * LANES] = out.astype(BF16)

    @pl.when(jnp.logical_and(j >= n_rope, j < n_rope + n_none))
    def _():
        z_ref[0] = acc.astype(BF16)

    @pl.when(j >= n_rope + n_none)
    def _():
        z_ref[0] = _sigmoid(acc).astype(BF16)


def _in_projection(xc, mods, g, cs, sn, w, wg, wgt, *, n_lat, tm, tn, na_w, n_rope, n_none):
    B, T, D = xc.shape
    NZ = w.shape[1]
    kern = functools.partial(_inproj_kernel, n_lat=n_lat, tm=tm, tn=tn, na_w=na_w, n_rope=n_rope,
                             n_none=n_none, q_scale=NA_HEAD_DIM ** -0.5)
    return pl.pallas_call(
        kern,
        grid=(B, T // tm, NZ // tn),
        in_specs=[pl.BlockSpec((1, tm, D), lambda b, i, j: (b, i, 0)),
                  pl.BlockSpec((1, 2, 6, D), lambda b, i, j: (b, 0, 0, 0)),
                  pl.BlockSpec((1, D), lambda b, i, j: (0, 0)),
                  pl.BlockSpec((tm, LANES), lambda b, i, j: (i, 0)),
                  pl.BlockSpec((tm, LANES), lambda b, i, j: (i, 0)),
                  pl.BlockSpec((D, tn), lambda b, i, j: (0, j)),
                  pl.BlockSpec((D, 2 * LANES), lambda b, i, j: (0, 0)),
                  pl.BlockSpec((16, D), lambda b, i, j: (0, 0))],
        out_specs=[pl.BlockSpec((1, tm, tn), lambda b, i, j: (b, i, j)),
                   pl.BlockSpec((1, tm, 2 * LANES), lambda b, i, j: (b, i, 0)),
                   pl.BlockSpec((1, 16, tm), lambda b, i, j: (b, 0, i))],
        out_shape=[jax.ShapeDtypeStruct((B, T, NZ), BF16),
                   jax.ShapeDtypeStruct((B, T, 2 * LANES), F32),
                   jax.ShapeDtypeStruct((B, 16, T), F32)],
        scratch_shapes=[pltpu.VMEM((tm, D), BF16)],
        compiler_params=_params(3),
        name="in_proj",
    )(xc, mods, g, cs, sn, w, wg, wgt)


def _na_kernel(q_ref, k_ref, v_ref, tab_ref, o_ref, *, n_lat, n_lat_tiles, win):
    g = pl.program_id(2)
    q = q_ref[0]
    kc = k_ref[0, pl.ds(n_lat, CHUNK), :]
    vc = v_ref[0, pl.ds(n_lat, CHUNK), :]
    s_ctx = _dot_nt(q, kc)

    @pl.when(g < n_lat_tiles)
    def _():
        start = pl.multiple_of(jnp.clip((g - 1) * CHUNK, 0, n_lat - win), CHUNK)
        kl = k_ref[0, pl.ds(start, win), :]
        vl = v_ref[0, pl.ds(start, win), :]
        s_loc = _dot_nt(q, kl) + tab_ref[0, 0]
        m = jnp.maximum(jnp.max(s_loc, axis=1, keepdims=True), jnp.max(s_ctx, axis=1, keepdims=True))
        e_loc = jnp.exp(s_loc - m)
        e_ctx = jnp.exp(s_ctx - m)
        denom = jnp.sum(e_loc, axis=1, keepdims=True) + jnp.sum(e_ctx, axis=1, keepdims=True)
        o = _dot(e_loc.astype(BF16), vl) + _dot(e_ctx.astype(BF16), vc)
        o_ref[0] = (o / denom).astype(BF16)

    @pl.when(g == n_lat_tiles)
    def _():
        m = jnp.max(s_ctx, axis=1, keepdims=True)
        e_ctx = jnp.exp(s_ctx - m)
        denom = jnp.sum(e_ctx, axis=1, keepdims=True)
        o_ref[0] = (_dot(e_ctx.astype(BF16), vc) / denom).astype(BF16)


def _na_table_type(g, n_lat_tiles):
    return jnp.where(g == 0, 0, jnp.where(g >= n_lat_tiles - 1, 2, 1))


def _neighbourhood_attention(z, tab, *, n_lat, n_heads, na_w):
    B, T, _ = z.shape
    n_lat_tiles = n_lat // CHUNK
    win = NA_KEY_ROWS * GRID_W
    kb = na_w // LANES
    kern = functools.partial(_na_kernel, n_lat=n_lat, n_lat_tiles=n_lat_tiles, win=win)
    return pl.pallas_call(
        kern,
        grid=(B, n_heads, n_lat_tiles + 1),
        in_specs=[pl.BlockSpec((1, CHUNK, LANES), lambda b, h, g: (b, g, h)),
                  pl.BlockSpec((1, T, LANES), lambda b, h, g: (b, 0, kb + h)),
                  pl.BlockSpec((1, T, LANES), lambda b, h, g: (b, 0, 2 * kb + h)),
                  pl.BlockSpec((1, 1, CHUNK, win), lambda b, h, g: (_na_table_type(g, n_lat_tiles), h, 0, 0))],
        out_specs=pl.BlockSpec((1, CHUNK, LANES), lambda b, h, g: (b, g, h)),
        out_shape=jax.ShapeDtypeStruct((B, T, na_w), BF16),
        compiler_params=_params(3),
        name="na_attn",
    )(z, z, z, tab)


def _na_bias_tables(rpb, n_rows):
    L, H = rpb.shape[:2]
    kw = NA_WIN_W
    n_tiles = n_rows // 4
    rq = np.arange(4)[:, None, None, None]
    cq = np.arange(GRID_W)[None, :, None, None]
    rk = np.arange(NA_KEY_ROWS)[None, None, :, None]
    ck = np.arange(GRID_W)[None, None, None, :]
    idx, valid = [], []
    for gt in (0, 1, n_tiles - 1):
        r = 4 * gt + rq
        ws = np.clip(4 * gt - 4, 0, n_rows - NA_KEY_ROWS)
        key_row = ws + rk
        row_start = np.clip(r - NA_WIN_H // 2, 0, n_rows - NA_WIN_H)
        ok_r = (key_row >= row_start) & (key_row < row_start + NA_WIN_H)
        dr = np.clip(key_row - r + NA_WIN_H - 1, 0, 2 * NA_WIN_H - 2)
        col_start = np.clip(cq - kw // 2, 0, GRID_W - kw)
        ok_c = (ck >= col_start) & (ck < col_start + kw)
        dc = np.clip(ck - cq, 1 - kw, kw - 1) + kw - 1
        idx.append(np.broadcast_to(dr * (2 * kw - 1) + dc, (4, GRID_W, NA_KEY_ROWS, GRID_W)))
        valid.append(np.broadcast_to(ok_r & ok_c, (4, GRID_W, NA_KEY_ROWS, GRID_W)))
    idx = np.stack(idx).reshape(3, CHUNK * NA_KEY_ROWS * GRID_W)
    valid = np.stack(valid).reshape(3, 1, CHUNK, NA_KEY_ROWS * GRID_W)
    flat = rpb.reshape(L, H, -1)
    tab = jnp.take(flat, jnp.asarray(idx.reshape(-1), jnp.int32), axis=2)
    tab = tab.reshape(L, H, 3, CHUNK, NA_KEY_ROWS * GRID_W).transpose(0, 2, 1, 3, 4)
    return jnp.where(jnp.asarray(valid)[None], tab, NEG_BIG)


def _ml_chunk(n, d, n_chunks):
    return jnp.where(n == 0, n_chunks - 1, jnp.where(d == 0, n - 1, n_chunks - 1 - n))


def _mlstm_kernel(qk_ref, hp_ref, hn_ref, v_ref, gcol_ref, grow_ref, cw_ref, cb_ref, gbc_ref, gbr_ref,
                  o_ref, xs_scr, c_scr, m_scr, *, n_chunks, n_heads):
    d = pl.program_id(1)
    n = pl.program_id(2)
    c = _ml_chunk(n, d, n_chunks)
    L = CHUNK
    qk_w = 2 * n_heads * ML_QK_DIM

    @pl.when(n == 0)
    def _():
        c_scr[...] = jnp.zeros_like(c_scr)
        m_scr[...] = jnp.full_like(m_scr, M_INIT)

    seq_start = jnp.logical_or(c == 0, c == n_chunks - 1)
    seq_end = c >= n_chunks - 2
    prev_row = jnp.where(seq_start, 0.0, hp_ref[0, BF16_ROWS - 1:BF16_ROWS, :].astype(F32))
    next_row = jnp.where(seq_end, 0.0, hn_ref[0, 0:1, :].astype(F32))
    xs_scr[pl.ds(8, L), :] = qk_ref[0].astype(F32)
    xs_scr[pl.ds(7, 1), :] = prev_row
    xs_scr[pl.ds(8 + L, 1), :] = next_row
    y = (cw_ref[0:1, :] * xs_scr[pl.ds(7, L), :] + cw_ref[1:2, :] * xs_scr[pl.ds(8, L), :]
         + cw_ref[2:3, :] * xs_scr[pl.ds(9, L), :] + cb_ref[...])
    y = y * _sigmoid(y)

    ga = gcol_ref[0] + gbc_ref[0]
    lf_c = _log_sigmoid(ga)
    gr = grow_ref[0] + gbr_ref[0][:, 0:1]
    lf_r = _log_sigmoid(gr)

    sgn = 1 - 2 * d
    ti = lax.broadcasted_iota(jnp.int32, (L, L), 0)
    si = lax.broadcasted_iota(jnp.int32, (L, L), 1)
    mask = ((si - ti) * sgn) <= 0
    maskb = mask.astype(F32).astype(BF16)
    hi, mid, lo = _split3(lf_c)
    b_c = _dot(maskb, hi) + _dot(maskb, mid) + _dot(maskb, lo)
    hi, mid, lo = _split3(jnp.concatenate([lf_r, jnp.zeros_like(lf_r)], axis=0))
    b_r = _dot_nt(hi, maskb) + _dot_nt(mid, maskb) + _dot_nt(lo, maskb)

    ones = jnp.ones((L, LANES), BF16)
    for hh in range(n_heads):
        q = (y[:, hh * ML_QK_DIM:(hh + 1) * ML_QK_DIM] * (ML_QK_DIM ** -0.5)).astype(BF16)
        kf = y[:, (n_heads + hh) * ML_QK_DIM:(n_heads + hh + 1) * ML_QK_DIM]
        vext = jnp.concatenate([v_ref[0, :, hh * ML_V_DIM:(hh + 1) * ML_V_DIM], ones], axis=1)
        li_c = ga[:, hh:hh + 1]
        b_col = b_c[:, n_heads + hh:n_heads + hh + 1]
        li_r = gr[hh:hh + 1, :]
        b_row = b_r[n_heads + hh:n_heads + hh + 1, :]
        g_tot = jnp.sum(lf_r[n_heads + hh:n_heads + hh + 1, :], axis=1, keepdims=True)
        m_prev = m_scr[hh, 0:1, 0:1]
        c_prev = c_scr[hh]

        dm = jnp.where(mask, b_col + (li_r - b_row), NEG_BIG)
        inter = b_col + m_prev
        m_t = jnp.maximum(inter, jnp.max(dm, axis=1, keepdims=True))
        w_inter = jnp.exp(inter - m_t)
        p = jnp.exp(dm - m_t)
        s = (_dot_nt(q, kf.astype(BF16)) * p).astype(BF16)
        r = _dot(s, vext) + w_inter * _dot(q, c_prev.astype(BF16))
        den = jnp.maximum(jnp.abs(r[:, ML_V_DIM:ML_V_DIM + 1]), jnp.exp(-m_t))
        o_ref[0, 0, :, hh * ML_V_DIM:(hh + 1) * ML_V_DIM] = (r[:, :ML_V_DIM] / den).astype(BF16)

        a_col = g_tot + li_c - b_col
        ma = jnp.max(a_col, axis=0, keepdims=True)
        wk = (jnp.exp(a_col - ma) * kf).astype(BF16)
        c_loc = _dot_tn(wk, vext)
        m_new = jnp.maximum(g_tot + m_prev, ma)
        dec = jnp.exp(g_tot + m_prev - m_new)
        inc = jnp.exp(ma - m_new)
        c_scr[hh] = dec * c_prev + inc * c_loc
        m_scr[hh] = jnp.broadcast_to(m_new, m_scr.shape[1:])


def _mlstm(z, gcol, grow, cw, cb, gbc, gbr, *, n_heads, qk_col, v_col):
    B, T, _ = z.shape
    n_chunks = T // CHUNK
    qk_w = 2 * n_heads * ML_QK_DIM
    v_w = n_heads * ML_V_DIM
    qk_blk = qk_col // qk_w
    v_blk = v_col // v_w
    hb = CHUNK // BF16_ROWS
    n_hblk = T // BF16_ROWS

    def chunk(n, d):
        return _ml_chunk(n, d, n_chunks)

    kern = functools.partial(_mlstm_kernel, n_chunks=n_chunks, n_heads=n_heads)
    return pl.pallas_call(
        kern,
        grid=(B, 2, n_chunks),
        in_specs=[pl.BlockSpec((1, CHUNK, qk_w), lambda b, d, n: (b, chunk(n, d), qk_blk)),
                  pl.BlockSpec((1, BF16_ROWS, qk_w),
                               lambda b, d, n: (b, jnp.maximum(chunk(n, d) * hb - 1, 0), qk_blk)),
                  pl.BlockSpec((1, BF16_ROWS, qk_w),
                               lambda b, d, n: (b, jnp.minimum((chunk(n, d) + 1) * hb, n_hblk - 1), qk_blk)),
                  pl.BlockSpec((1, CHUNK, v_w), lambda b, d, n: (b, chunk(n, d), v_blk)),
                  pl.BlockSpec((1, CHUNK, LANES), lambda b, d, n: (b, chunk(n, d), d)),
                  pl.BlockSpec((1, 8, CHUNK), lambda b, d, n: (b, d, chunk(n, d))),
                  pl.BlockSpec((3, qk_w), lambda b, d, n: (0, 0)),
                  pl.BlockSpec((1, qk_w), lambda b, d, n: (0, 0)),
                  pl.BlockSpec((1, 1, LANES), lambda b, d, n: (d, 0, 0)),
                  pl.BlockSpec((1, 8, LANES), lambda b, d, n: (d, 0, 0))],
        out_specs=pl.BlockSpec((1, 1, CHUNK, v_w), lambda b, d, n: (d, b, chunk(n, d), 0)),
        out_shape=jax.ShapeDtypeStruct((2, B, T, v_w), BF16),
        scratch_shapes=[pltpu.VMEM((CHUNK + 16, qk_w), F32),
                        pltpu.VMEM((n_heads, ML_QK_DIM, ML_V_DIM + LANES), F32),
                        pltpu.VMEM((n_heads, 8, LANES), F32)],
        compiler_params=_params(3),
        name="mlstm",
    )(z, z, z, z, gcol, grow, cw, cb, gbc, gbr)


def _gated_residual(x, y, post_g, gate):
    var = jnp.mean(y * y, axis=-1, keepdims=True)
    return x + gate * (y * lax.rsqrt(var + EPS) * post_g)


def _merge_kernel(na_ref, hf_ref, hb_ref, mo_ref, gn_ref, gm_ref, x_ref, mod_ref, ng_ref, pg_ref,
                  wna_ref, wml_ref, wo_ref, o_ref, *, n_heads):
    hs = hf_ref[0, 0].astype(F32) + hb_ref[0, 0].astype(F32)
    parts = []
    for hh in range(n_heads):
        hv = hs[:, hh * ML_V_DIM:(hh + 1) * ML_V_DIM]
        var = jnp.mean(hv * hv, axis=-1, keepdims=True)
        parts.append(hv * lax.rsqrt(var + EPS))
    ml = jnp.concatenate(parts, axis=1) * ng_ref[...] * mo_ref[0].astype(F32)
    y = (gn_ref[0].astype(F32) * _dot(na_ref[0], wna_ref[...])
         + gm_ref[0].astype(F32) * _dot(ml.astype(BF16), wml_ref[...]))
    yo = _dot(y.astype(BF16), wo_ref[...])
    o_ref[0] = _gated_residual(x_ref[0], yo, pg_ref[...], mod_ref[0, 0, 2:3, :])


def _merge(na_o, ml_h, z, xc, mods, ng, pg, wna, wml, wo, *, n_heads, mo_col, gn_col, gm_col):
    B, T, D = xc.shape
    na_w = na_o.shape[2]
    v_w = ml_h.shape[3]
    nt = T // CHUNK
    kern = functools.partial(_merge_kernel, n_heads=n_heads)
    const = dict(pipeline_mode=pl.Buffered(1))
    return pl.pallas_call(
        kern,
        grid=(B, nt),
        in_specs=[pl.BlockSpec((1, CHUNK, na_w), lambda b, i: (b, i, 0)),
                  pl.BlockSpec((1, 1, CHUNK, v_w), lambda b, i: (0, b, i, 0)),
                  pl.BlockSpec((1, 1, CHUNK, v_w), lambda b, i: (1, b, i, 0)),
                  pl.BlockSpec((1, CHUNK, v_w), lambda b, i: (b, i, mo_col // v_w)),
                  pl.BlockSpec((1, CHUNK, D), lambda b, i: (b, i, gn_col // D)),
                  pl.BlockSpec((1, CHUNK, D), lambda b, i: (b, i, gm_col // D)),
                  pl.BlockSpec((1, CHUNK, D), lambda b, i: (b, i, 0)),
                  pl.BlockSpec((1, 1, 6, D), lambda b, i: (b, i // (nt - 1), 0, 0)),
                  pl.BlockSpec((1, v_w), lambda b, i: (0, 0)),
                  pl.BlockSpec((1, D), lambda b, i: (0, 0)),
                  pl.BlockSpec((na_w, D), lambda b, i: (0, 0), **const),
                  pl.BlockSpec((v_w, D), lambda b, i: (0, 0), **const),
                  pl.BlockSpec((D, D), lambda b, i: (0, 0), **const)],
        out_specs=pl.BlockSpec((1, CHUNK, D), lambda b, i: (b, i, 0)),
        out_shape=jax.ShapeDtypeStruct((B, T, D), F32),
        compiler_params=_params(2),
        name="merge_out_proj",
    )(na_o, ml_h, ml_h, z, z, z, xc, mods, ng, pg, wna, wml, wo)


def _ffn_up_kernel(x_ref, xp_ref, xn_ref, mod_ref, g_ref, wg_ref, wv_ref, cw_ref, cb_ref,
                   o_ref, h_scr, u_scr, *, n_lat, n_tot, tm):
    i = pl.program_id(1)
    j = pl.program_id(2)
    H = BF16_ROWS

    @pl.when(j == 0)
    def _():
        g = g_ref[...]
        h_scr[pl.ds(0, H), :] = _norm_mod(xp_ref[0], g, mod_ref, i * tm - H, n_lat, 3).astype(BF16)
        h_scr[pl.ds(H, tm), :] = _norm_mod(x_ref[0], g, mod_ref, i * tm, n_lat, 3).astype(BF16)
        h_scr[pl.ds(H + tm, H), :] = _norm_mod(xn_ref[0], g, mod_ref, i * tm + tm, n_lat, 3).astype(BF16)

    u_scr[...] = _dot(h_scr[...], wg_ref[...])
    val = _dot(h_scr[pl.ds(H, tm), :], wv_ref[...])
    row = i * tm + lax.broadcasted_iota(jnp.int32, (tm, 1), 0)
    is_start = jnp.logical_or(row == 0, row == n_lat)
    is_end = jnp.logical_or(row == n_lat - 1, row == n_tot - 1)
    u_prev = jnp.where(is_start, 0.0, u_scr[pl.ds(H - 1, tm), :])
    u_next = jnp.where(is_end, 0.0, u_scr[pl.ds(H + 1, tm), :])
    y = cw_ref[0:1, :] * u_prev + cw_ref[1:2, :] * u_scr[pl.ds(H, tm), :] + cw_ref[2:3, :] * u_next + cb_ref[...]
    o_ref[0] = (y * _sigmoid(y) * val).astype(BF16)


def _ffn_up(xc, mods, g, w_up, cw, cb, *, n_lat, tm, tn):
    B, T, D = xc.shape
    dff = w_up.shape[1] // 2
    nj = dff // tn
    hb = tm // BF16_ROWS
    n_hblk = T // BF16_ROWS
    kern = functools.partial(_ffn_up_kernel, n_lat=n_lat, n_tot=T, tm=tm)
    return pl.pallas_call(
        kern,
        grid=(B, T // tm, nj),
        in_specs=[pl.BlockSpec((1, tm, D), lambda b, i, j: (b, i, 0)),
                  pl.BlockSpec((1, BF16_ROWS, D), lambda b, i, j: (b, jnp.maximum(i * hb - 1, 0), 0)),
                  pl.BlockSpec((1, BF16_ROWS, D), lambda b, i, j: (b, jnp.minimum((i + 1) * hb, n_hblk - 1), 0)),
                  pl.BlockSpec((1, 2, 6, D), lambda b, i, j: (b, 0, 0, 0)),
                  pl.BlockSpec((1, D), lambda b, i, j: (0, 0)),
                  pl.BlockSpec((D, tn), lambda b, i, j: (0, j)),
                  pl.BlockSpec((D, tn), lambda b, i, j: (0, nj + j)),
                  pl.BlockSpec((3, tn), lambda b, i, j: (0, j)),
                  pl.BlockSpec((1, tn), lambda b, i, j: (0, j))],
        out_specs=pl.BlockSpec((1, tm, tn), lambda b, i, j: (b, i, j)),
        out_shape=jax.ShapeDtypeStruct((B, T, dff), BF16),
        scratch_shapes=[pltpu.VMEM((tm + 2 * BF16_ROWS, D), BF16),
                        pltpu.VMEM((tm + 2 * BF16_ROWS, tn), F32)],
        compiler_params=_params(3),
        name="ffn_up",
    )(xc, xc, xc, mods, g, w_up, w_up, cw, cb)


def _ffn_down_kernel(a_ref, x_ref, mod_ref, pg_ref, w_ref, o_ref):
    y = _dot(a_ref[0], w_ref[...])
    o_ref[0] = _gated_residual(x_ref[0], y, pg_ref[...], mod_ref[0, 0, 5:6, :])


def _ffn_down(act, xc, mods, pg, w_down):
    B, T, D = xc.shape
    dff = act.shape[2]
    nt = T // CHUNK
    return pl.pallas_call(
        _ffn_down_kernel,
        grid=(B, nt),
        in_specs=[pl.BlockSpec((1, CHUNK, dff), lambda b, i: (b, i, 0)),
                  pl.BlockSpec((1, CHUNK, D), lambda b, i: (b, i, 0)),
                  pl.BlockSpec((1, 1, 6, D), lambda b, i: (b, i // (nt - 1), 0, 0)),
                  pl.BlockSpec((1, D), lambda b, i: (0, 0)),
                  pl.BlockSpec((dff, D), lambda b, i: (0, 0), pipeline_mode=pl.Buffered(1))],
        out_specs=pl.BlockSpec((1, CHUNK, D), lambda b, i: (b, i, 0)),
        out_shape=jax.ShapeDtypeStruct((B, T, D), F32),
        compiler_params=_params(2),
        name="ffn_down",
    )(act, xc, mods, pg, w_down)


def _rope_tables(n_lat, n_ctx):
    t = np.arange(n_lat)
    row = (t // GRID_W).astype(np.float32)
    col = (t % GRID_W).astype(np.float32)
    n_freq = NA_HEAD_DIM // 4
    inv_freq = jnp.asarray(ROPE_THETA, F32) ** (-jnp.arange(n_freq, dtype=F32) / n_freq)
    ang = jnp.concatenate([jnp.asarray(row)[:, None] * inv_freq, jnp.asarray(col)[:, None] * inv_freq], axis=-1)
    cos, sin = jnp.cos(ang), jnp.sin(ang)
    cs = jnp.concatenate([cos, cos], axis=1)
    sn = jnp.concatenate([-sin, sin], axis=1)
    cs = jnp.concatenate([cs, jnp.ones((n_ctx, NA_HEAD_DIM), F32)], axis=0)
    sn = jnp.concatenate([sn, jnp.zeros((n_ctx, NA_HEAD_DIM), F32)], axis=0)
    return cs, sn


def kernel(x, c, ctx, c_ctx, w_mod, b_mod, norm_mix_pre, norm_mix_post, norm_ffn_pre, norm_ffn_post, w_in, na_rpb, ml_conv_w, ml_conv_b, ml_igate_b, ml_fgate_b, ml_norm_g, w_na_proj, w_ml_proj, w_out, w_up, ffn_conv_w, ffn_conv_b, w_down):
    B, S, D = x.shape
    CL = ctx.shape[1]
    T = S + CL
    L = w_mod.shape[0]
    na_w = w_na_proj.shape[1]
    v_w = w_ml_proj.shape[1]
    n_na_heads = na_w // NA_HEAD_DIM
    n_ml_heads = v_w // ML_V_DIM
    qk_w = 2 * n_ml_heads * ML_QK_DIM
    n_gate = 2 * n_ml_heads
    dff = w_down.shape[1]
    assert CL == CHUNK and S % CHUNK == 0 and S % GRID_W == 0 and S // GRID_W >= NA_KEY_ROWS
    assert n_ml_heads <= 4 and B + 1 <= MOD_ROWS
    gate_col = 3 * na_w + qk_w + 2 * v_w
    assert w_in.shape[2] == gate_col + 2 * n_gate + 2 * D

    n_rope_cols = 2 * na_w
    n_none_cols = na_w + qk_w + v_w
    n_sig_cols = v_w + 2 * D
    tn = next(t for t in (1024, 512, 256, 128)
              if n_rope_cols % t == 0 and n_none_cols % t == 0 and n_sig_cols % t == 0)
    qk_col = 3 * na_w
    v_col = qk_col + qk_w
    mo_col = v_col + v_w
    gn_col = mo_col + v_w
    gm_col = gn_col + D
    assert qk_col % qk_w == 0 and v_col % v_w == 0 and mo_col % v_w == 0 and gn_col % D == 0
    tm = _pick_tile(T, (768, 256))
    tn_ff = _pick_tile(dff, (512, 256, 128))

    crow = jnp.concatenate([c, c_ctx[None], jnp.zeros((MOD_ROWS - B - 1, D), F32)], axis=0)
    mod_all = _modulation(crow, w_mod, b_mod).reshape(L, MOD_ROWS, 6, D)
    mods_all = jnp.stack([mod_all[:, :B], jnp.broadcast_to(mod_all[:, B:B + 1], (L, B, 6, D))], axis=2)

    cs, sn = _rope_tables(S, CL)
    tabs = _na_bias_tables(na_rpb, S // GRID_W)

    gb = jnp.concatenate([ml_igate_b, ml_fgate_b], axis=2)
    gbc_all = jnp.pad(gb, ((0, 0), (0, 0), (0, LANES - n_gate)))[:, :, None, :]
    gbr_all = jnp.broadcast_to(jnp.pad(gb, ((0, 0), (0, 0), (0, 8 - n_gate)))[..., None], (L, 2, 8, LANES))

    xc = jnp.concatenate([x, ctx], axis=1)
    for l in range(L):
        mods = mods_all[l]
        wl = w_in[l]
        w_main = jnp.concatenate([wl[:, :gate_col], wl[:, gate_col + 2 * n_gate:]], axis=1).astype(BF16)
        ig = wl[:, gate_col:gate_col + n_gate].reshape(D, 2, n_ml_heads)
        fg = wl[:, gate_col + n_gate:gate_col + 2 * n_gate].reshape(D, 2, n_ml_heads)
        gcols = jnp.concatenate([ig, fg], axis=2)
        wg = jnp.pad(gcols, ((0, 0), (0, 0), (0, LANES - n_gate))).reshape(D, 2 * LANES).astype(BF16)
        wgt = jnp.pad(gcols, ((0, 0), (0, 0), (0, 8 - n_gate))).reshape(D, 16).T.astype(BF16)

        z, gcol, grow = _in_projection(
            xc, mods, norm_mix_pre[l][None], cs, sn, w_main, wg, wgt,
            n_lat=S, tm=tm, tn=tn, na_w=na_w, n_rope=n_rope_cols // tn, n_none=n_none_cols // tn)
        na_o = _neighbourhood_attention(z, tabs[l], n_lat=S, n_heads=n_na_heads, na_w=na_w)
        ml_h = _mlstm(z, gcol, grow, ml_conv_w[l], ml_conv_b[l][None], gbc_all[l], gbr_all[l],
                      n_heads=n_ml_heads, qk_col=qk_col, v_col=v_col)
        xc = _merge(na_o, ml_h, z, xc, mods, ml_norm_g[l][None], norm_mix_post[l][None],
                    w_na_proj[l].astype(BF16), w_ml_proj[l].astype(BF16), w_out[l].astype(BF16),
                    n_heads=n_ml_heads, mo_col=mo_col, gn_col=gn_col, gm_col=gm_col)
        act = _ffn_up(xc, mods, norm_ffn_pre[l][None], w_up[l].astype(BF16), ffn_conv_w[l], ffn_conv_b[l][None],
                      n_lat=S, tm=tm, tn=tn_ff)
        xc = _ffn_down(act, xc, mods, norm_ffn_post[l][None], w_down[l].astype(BF16))
    return xc[:, :S]
```

```python
import functools

import numpy as np
import jax
import jax.numpy as jnp
from jax import lax
from jax.experimental import pallas as pl
from jax.experimental.pallas import tpu as pltpu

GRID_W = 64
NA_HEAD_DIM = 128
NA_WIN_H = 8
NA_WIN_W = 16
ROPE_THETA = 10000.0
ML_QK_DIM = 128
ML_V_DIM = 256
EPS = 1e-6
M_INIT = -1e30
NEG_BIG = -1e30

LANES = 128
MXU_COLS = 256
BF16_ROWS = 16
MOD_ROWS = 16
CHUNK = 256
NA_KEY_ROWS = 12
VMEM_LIMIT = 56 * 1024 * 1024

F32 = jnp.float32
BF16 = jnp.bfloat16


def _dot(a, b):
    return jnp.dot(a, b, preferred_element_type=F32)


def _dot_nt(a, b):
    return lax.dot_general(a, b, (((1,), (1,)), ((), ())), preferred_element_type=F32)


def _dot_tn(a, b):
    return lax.dot_general(a, b, (((0,), (0,)), ((), ())), preferred_element_type=F32)


def _sigmoid(x):
    return 0.5 * jnp.tanh(0.5 * x) + 0.5


def _log_sigmoid(x):
    return jnp.minimum(x, 0.0) - jnp.log(1.0 + jnp.exp(-jnp.abs(x)))


def _split3(x):
    hi = x.astype(BF16)
    r1 = x - hi.astype(F32)
    mid = r1.astype(BF16)
    lo = (r1 - mid.astype(F32)).astype(BF16)
    return hi, mid, lo


def _params(n_grid):
    return pltpu.CompilerParams(dimension_semantics=("arbitrary",) * n_grid,
                                vmem_limit_bytes=VMEM_LIMIT)


def _pick_tile(n, candidates):
    for c in candidates:
        if n % c == 0:
            return c
    raise ValueError(f"no tile for {n} among {candidates}")


def _mod_kernel(c_ref, w_ref, b_ref, o_ref):
    c = c_ref[...]
    s = c * _sigmoid(c)
    hi, mid, lo = _split3(s)
    whi, wmid, wlo = _split3(w_ref[0])
    acc = _dot(hi, whi) + (_dot(hi, wmid) + _dot(mid, whi)) + (_dot(hi, wlo) + _dot(mid, wmid) + _dot(lo, whi))
    o_ref[0] = acc + b_ref[0]


def _modulation(crow, w_mod, b_mod):
    L, D, N = w_mod.shape
    tn = _pick_tile(N, (1024, 512, 256, 128))
    return pl.pallas_call(
        _mod_kernel,
        grid=(L, N // tn),
        in_specs=[pl.BlockSpec((MOD_ROWS, D), lambda l, j: (0, 0)),
                  pl.BlockSpec((1, D, tn), lambda l, j: (l, 0, j)),
                  pl.BlockSpec((1, 1, tn), lambda l, j: (l, 0, j))],
        out_specs=pl.BlockSpec((1, MOD_ROWS, tn), lambda l, j: (l, 0, j)),
        out_shape=jax.ShapeDtypeStruct((L, MOD_ROWS, N), F32),
        compiler_params=_params(2),
        name="adaln_mod",
    )(crow, w_mod, b_mod.reshape(L, 1, N))


def _norm_mod(x, g, mod_ref, kind, which):
    var = jnp.mean(x * x, axis=-1, keepdims=True)
    y = x * lax.rsqrt(var + EPS) * g
    return (y * (1.0 + mod_ref[0, kind, which + 1:which + 2, :]) + mod_ref[0, kind, which:which + 1, :]).astype(BF16)


def _fill_h(h_scr, x_ref, xp_ref, xn_ref, mod_ref, g, i, *, tm, n_lat, n_tiles, which):
    H = BF16_ROWS
    split = n_lat - (n_tiles - 1) * tm
    h_scr[pl.ds(0, H), :] = _norm_mod(xp_ref[0], g, mod_ref, 0, which)
    h_scr[pl.ds(H + tm, H), :] = _norm_mod(xn_ref[0], g, mod_ref, 0, which)

    @pl.when(i < n_tiles - 1)
    def _():
        h_scr[pl.ds(H, tm), :] = _norm_mod(x_ref[0], g, mod_ref, 0, which)

    @pl.when(i == n_tiles - 1)
    def _():
        if split > 0:
            h_scr[pl.ds(H, split), :] = _norm_mod(x_ref[0, 0:split, :], g, mod_ref, 0, which)
        if split < tm:
            h_scr[pl.ds(H + split, tm - split), :] = _norm_mod(x_ref[0, split:tm, :], g, mod_ref, 1, which)


def _seq_edges(i, tm, n_lat, n_tot):
    row = i * tm + lax.broadcasted_iota(jnp.int32, (tm, 1), 0)
    is_start = jnp.logical_or(row == 0, row == n_lat)
    is_end = jnp.logical_or(row == n_lat - 1, row == n_tot - 1)
    return is_start, is_end


def _conv3(u_ref, width, cw, cb, is_start, is_end, tm):
    H = BF16_ROWS
    u_prev = jnp.where(is_start, 0.0, u_ref[pl.ds(H - 1, tm), pl.ds(0, width)])
    u_next = jnp.where(is_end, 0.0, u_ref[pl.ds(H + 1, tm), pl.ds(0, width)])
    return cw[0:1, :] * u_prev + cw[1:2, :] * u_ref[pl.ds(H, tm), pl.ds(0, width)] + cw[2:3, :] * u_next + cb


def _halo_specs(tm, T, D):
    hb = tm // BF16_ROWS
    n_hblk = T // BF16_ROWS
    return [pl.BlockSpec((1, BF16_ROWS, D), lambda b, i, j: (b, jnp.maximum(i * hb - 1, 0), 0)),
            pl.BlockSpec((1, BF16_ROWS, D), lambda b, i, j: (b, jnp.minimum((i + 1) * hb, n_hblk - 1), 0))]


def _inproj_kernel(x_ref, xp_ref, xn_ref, mod_ref, g_ref, cs_ref, sn_ref, w_ref, wg_ref, wgt_ref,
                   cw_ref, cb_ref, qs_ref, z_ref, gcol_ref, grow_ref, h_scr, u_scr, *,
                   n_lat, n_tot, tm, tn, n_col, sub, na_w, qk_col, v_col, mo_col, q_scale):
    i = pl.program_id(1)
    j = pl.program_id(2)
    H = BF16_ROWS

    @pl.when(j == 0)
    def _():
        _fill_h(h_scr, x_ref, xp_ref, xn_ref, mod_ref, g_ref[...], i,
                tm=tm, n_lat=n_lat, n_tiles=n_tot // tm, which=0)
        h = h_scr[pl.ds(H, tm), :]
        gcol_ref[0] = _dot(h, wg_ref[...])
        grow_ref[0] = _dot_nt(wgt_ref[...], h)

    def col_tile(jj):
        is_start, is_end = _seq_edges(i, tm, n_lat, n_tot)
        for sc in range(tn // sub):
            col = jj * tn + sc * sub
            cols = slice(sc * sub, (sc + 1) * sub)
            if qk_col <= col < v_col:
                u = u_scr.at[(col - qk_col) // sub]
                u[...] = _dot(h_scr[...], w_ref[:, cols])
                cc = slice(col - qk_col, col - qk_col + sub)
                y = _conv3(u, sub, cw_ref[:, cc], cb_ref[:, cc], is_start, is_end, tm)
                z_ref[0, :, cols] = (y * _sigmoid(y) * qs_ref[:, cc]).astype(BF16)
                continue
            acc = _dot(h_scr[pl.ds(H, tm), :], w_ref[:, cols])
            if col < 2 * na_w:
                for c in range(sub // LANES):
                    chunk = acc[:, c * LANES:(c + 1) * LANES]
                    out = chunk * cs_ref[...] + pltpu.roll(chunk, LANES // 2, axis=1) * sn_ref[...]
                    if col < na_w:
                        out = out * q_scale
                    z_ref[0, :, sc * sub + c * LANES:sc * sub + (c + 1) * LANES] = out.astype(BF16)
            elif col < mo_col:
                z_ref[0, :, cols] = acc.astype(BF16)
            else:
                z_ref[0, :, cols] = _sigmoid(acc).astype(BF16)

    for jj in range(n_col):
        pl.when(j == jj)(functools.partial(col_tile, jj))


def _in_projection(xc, mods, g, cs, sn, w, wg, wgt, cw, cb, qs, *, n_lat, tm, tn, na_w, qk_col, v_col, mo_col):
    B, T, D = xc.shape
    NZ = w.shape[1]
    sub = min(tn, MXU_COLS)
    qk_w = v_col - qk_col
    kern = functools.partial(_inproj_kernel, n_lat=n_lat, n_tot=T, tm=tm, tn=tn, n_col=NZ // tn, sub=sub, na_w=na_w,
                             qk_col=qk_col, v_col=v_col, mo_col=mo_col, q_scale=NA_HEAD_DIM ** -0.5)

    return pl.pallas_call(
        kern,
        grid=(B, T // tm, NZ // tn),
        in_specs=[pl.BlockSpec((1, tm, D), lambda b, i, j: (b, i, 0))] + _halo_specs(tm, T, D) + [
                  pl.BlockSpec((1, 2, 6, D), lambda b, i, j: (b, 0, 0, 0)),
                  pl.BlockSpec((1, D), lambda b, i, j: (0, 0)),
                  pl.BlockSpec((tm, LANES), lambda b, i, j: (i, 0)),
                  pl.BlockSpec((tm, LANES), lambda b, i, j: (i, 0)),
                  pl.BlockSpec((D, tn), lambda b, i, j: (0, j)),
                  pl.BlockSpec((D, 2 * LANES), lambda b, i, j: (0, 0)),
                  pl.BlockSpec((16, D), lambda b, i, j: (0, 0)),
                  pl.BlockSpec((3, qk_w), lambda b, i, j: (0, 0)),
                  pl.BlockSpec((1, qk_w), lambda b, i, j: (0, 0)),
                  pl.BlockSpec((1, qk_w), lambda b, i, j: (0, 0))],
        out_specs=[pl.BlockSpec((1, tm, tn), lambda b, i, j: (b, i, j)),
                   pl.BlockSpec((1, tm, 2 * LANES), lambda b, i, j: (b, i, 0)),
                   pl.BlockSpec((1, 16, tm), lambda b, i, j: (b, 0, i))],
        out_shape=[jax.ShapeDtypeStruct((B, T, NZ), BF16),
                   jax.ShapeDtypeStruct((B, T, 2 * LANES), F32),
                   jax.ShapeDtypeStruct((B, 16, T), F32)],
        scratch_shapes=[pltpu.VMEM((tm + 2 * BF16_ROWS, D), BF16),
                        pltpu.VMEM((qk_w // sub, tm + 2 * BF16_ROWS, sub), F32)],
        compiler_params=_params(3),
        name="in_proj",
    )(xc, xc, xc, mods, g, cs, sn, w, wg, wgt, cw, cb, qs)


def _na_kernel(q_ref, k_ref, v_ref, tab_ref, o_ref, *, n_lat, n_lat_tiles, win, hpb):
    g = pl.program_id(2)

    @pl.when(g < n_lat_tiles)
    def _():
        start = pl.multiple_of(jnp.clip((g - 1) * CHUNK, 0, n_lat - win), CHUNK)
        for hh in range(hpb):
            hs = slice(hh * LANES, (hh + 1) * LANES)
            q = q_ref[0, :, hs]
            kc = k_ref[0, pl.ds(n_lat, CHUNK), hs]
            vc = v_ref[0, pl.ds(n_lat, CHUNK), hs]
            kl = k_ref[0, pl.ds(start, win), hs]
            vl = v_ref[0, pl.ds(start, win), hs]
            s_ctx = _dot_nt(q, kc)
            s_loc = _dot_nt(q, kl) + tab_ref[0, hh]
            m = jnp.maximum(jnp.max(s_loc, axis=1, keepdims=True), jnp.max(s_ctx, axis=1, keepdims=True))
            e_loc = jnp.exp(s_loc - m)
            e_ctx = jnp.exp(s_ctx - m)
            denom = jnp.sum(e_loc, axis=1, keepdims=True) + jnp.sum(e_ctx, axis=1, keepdims=True)
            o = _dot(e_loc.astype(BF16), vl) + _dot(e_ctx.astype(BF16), vc)
            o_ref[0, :, hs] = (o / denom).astype(BF16)

    @pl.when(g == n_lat_tiles)
    def _():
        for hh in range(hpb):
            hs = slice(hh * LANES, (hh + 1) * LANES)
            vc = v_ref[0, pl.ds(n_lat, CHUNK), hs]
            s_ctx = _dot_nt(q_ref[0, :, hs], k_ref[0, pl.ds(n_lat, CHUNK), hs])
            e_ctx = jnp.exp(s_ctx - jnp.max(s_ctx, axis=1, keepdims=True))
            denom = jnp.sum(e_ctx, axis=1, keepdims=True)
            o_ref[0, :, hs] = (_dot(e_ctx.astype(BF16), vc) / denom).astype(BF16)


def _na_table_type(g, n_lat_tiles):
    return jnp.where(g == 0, 0, jnp.where(g >= n_lat_tiles - 1, 2, 1))


def _neighbourhood_attention(z, tab, *, n_lat, n_heads, na_w):
    B, T, _ = z.shape
    n_lat_tiles = n_lat // CHUNK
    win = NA_KEY_ROWS * GRID_W
    hpb = _pick_tile(n_heads, (4, 2, 1))
    n_hg = n_heads // hpb
    kern = functools.partial(_na_kernel, n_lat=n_lat, n_lat_tiles=n_lat_tiles, win=win, hpb=hpb)
    return pl.pallas_call(
        kern,
        grid=(B, n_hg, n_lat_tiles + 1),
        in_specs=[pl.BlockSpec((1, CHUNK, hpb * LANES), lambda b, h, g: (b, g, h)),
                  pl.BlockSpec((1, T, hpb * LANES), lambda b, h, g: (b, 0, n_hg + h)),
                  pl.BlockSpec((1, T, hpb * LANES), lambda b, h, g: (b, 0, 2 * n_hg + h)),
                  pl.BlockSpec((1, hpb, CHUNK, win), lambda b, h, g: (_na_table_type(g, n_lat_tiles), h, 0, 0))],
        out_specs=pl.BlockSpec((1, CHUNK, hpb * LANES), lambda b, h, g: (b, g, h)),
        out_shape=jax.ShapeDtypeStruct((B, T, na_w), BF16),
        compiler_params=_params(3),
        name="na_attn",
    )(z, z, z, tab)


def _na_bias_tables(rpb, n_rows):
    L, H = rpb.shape[:2]
    kw = NA_WIN_W
    n_tiles = n_rows // 4
    cq = np.arange(GRID_W)[:, None]
    ck = np.arange(GRID_W)[None, :]
    dc = np.clip(ck - cq, 1 - kw, kw - 1) + kw - 1
    col_start = np.clip(cq - kw // 2, 0, GRID_W - kw)
    ok_c = (ck >= col_start) & (ck < col_start + kw)
    onehot = (dc[..., None] == np.arange(2 * kw - 1)).astype(np.float32)
    toe = jnp.einsum("lhdj,qkj->lhdqk", rpb, jnp.asarray(onehot), precision=lax.Precision.HIGHEST)
    toe = jnp.where(jnp.asarray(ok_c), toe, NEG_BIG)
    masked = 2 * NA_WIN_H - 1
    toe = jnp.concatenate([toe, jnp.full((L, H, 1, GRID_W, GRID_W), NEG_BIG, F32)], axis=2)
    types = []
    for gt in (0, 1, n_tiles - 1):
        ws = int(np.clip(4 * gt - 4, 0, n_rows - NA_KEY_ROWS))
        per_rq = []
        for rq in range(4):
            r = 4 * gt + rq
            row_start = int(np.clip(r - NA_WIN_H // 2, 0, n_rows - NA_WIN_H))
            blocks = []
            for rk in range(NA_KEY_ROWS):
                key_row = ws + rk
                ok_r = row_start <= key_row < row_start + NA_WIN_H
                blocks.append(toe[:, :, key_row - r + NA_WIN_H - 1 if ok_r else masked])
            per_rq.append(jnp.stack(blocks, axis=3))
        types.append(jnp.stack(per_rq, axis=2))
    tab = jnp.stack(types, axis=1)
    return tab.reshape(L, 3, H, CHUNK, NA_KEY_ROWS * GRID_W)


def _ml_chunk(n, d, n_chunks):
    return jnp.where(n == 0, n_chunks - 1, jnp.where(d == 0, n - 1, n_chunks - 1 - n))


def _mlstm_kernel(qk_ref, v_ref, gcol_ref, grow_ref, gbc_ref, gbr_ref, o_ref, c_scr, m_scr, *, n_heads):
    d = pl.program_id(1)
    n = pl.program_id(2)
    L = CHUNK

    @pl.when(n == 0)
    def _():
        c_scr[...] = jnp.zeros_like(c_scr)
        m_scr[...] = jnp.full_like(m_scr, M_INIT)

    ga = gcol_ref[0] + gbc_ref[0]
    lf_c = _log_sigmoid(ga)
    gr = grow_ref[0] + gbr_ref[0][:, 0:1]
    lf_r = _log_sigmoid(gr)

    sgn = 1 - 2 * d
    ti = lax.broadcasted_iota(jnp.int32, (L, L), 0)
    si = lax.broadcasted_iota(jnp.int32, (L, L), 1)
    mask = ((si - ti) * sgn) <= 0
    maskb = mask.astype(F32).astype(BF16)
    hi, mid, lo = _split3(lf_c)
    b_c = _dot(maskb, hi) + _dot(maskb, mid) + _dot(maskb, lo)
    hi, mid, lo = _split3(jnp.concatenate([lf_r, jnp.zeros_like(lf_r)], axis=0))
    b_r = _dot_nt(hi, maskb) + _dot_nt(mid, maskb) + _dot_nt(lo, maskb)

    ones = jnp.ones((L, LANES), BF16)
    for hh in range(n_heads):
        q = qk_ref[0, :, hh * ML_QK_DIM:(hh + 1) * ML_QK_DIM]
        k = qk_ref[0, :, (n_heads + hh) * ML_QK_DIM:(n_heads + hh + 1) * ML_QK_DIM]
        vext = jnp.concatenate([v_ref[0, :, hh * ML_V_DIM:(hh + 1) * ML_V_DIM], ones], axis=1)
        li_c = ga[:, hh:hh + 1]
        b_col = b_c[:, n_heads + hh:n_heads + hh + 1]
        li_r = gr[hh:hh + 1, :]
        b_row = b_r[n_heads + hh:n_heads + hh + 1, :]
        g_tot = jnp.sum(lf_r[n_heads + hh:n_heads + hh + 1, :], axis=1, keepdims=True)
        m_prev = m_scr[hh, 0:1, 0:1]
        c_prev = c_scr[hh]

        dm = jnp.where(mask, b_col + (li_r - b_row), NEG_BIG)
        inter = b_col + m_prev
        m_t = jnp.maximum(inter, jnp.max(dm, axis=1, keepdims=True))
        w_inter = jnp.exp(inter - m_t)
        p = jnp.exp(dm - m_t)
        s = (_dot_nt(q, k) * p).astype(BF16)
        r = _dot(s, vext) + w_inter * _dot(q, c_prev.astype(BF16))
        den = jnp.maximum(jnp.abs(r[:, ML_V_DIM:ML_V_DIM + 1]), jnp.exp(-m_t))
        o_ref[0, 0, :, hh * ML_V_DIM:(hh + 1) * ML_V_DIM] = (r[:, :ML_V_DIM] / den).astype(BF16)

        a_col = g_tot + li_c - b_col
        ma = jnp.max(a_col, axis=0, keepdims=True)
        wk = (jnp.exp(a_col - ma) * k.astype(F32)).astype(BF16)
        c_loc = _dot_tn(wk, vext)
        m_new = jnp.maximum(g_tot + m_prev, ma)
        dec = jnp.exp(g_tot + m_prev - m_new)
        inc = jnp.exp(ma - m_new)
        c_scr[hh] = dec * c_prev + inc * c_loc
        m_scr[hh] = jnp.broadcast_to(m_new, m_scr.shape[1:])


def _mlstm(z, gcol, grow, gbc, gbr, *, n_heads, qk_col, v_col):
    B, T, _ = z.shape
    n_chunks = T // CHUNK
    qk_w = 2 * n_heads * ML_QK_DIM
    v_w = n_heads * ML_V_DIM
    qk_blk = qk_col // qk_w
    v_blk = v_col // v_w

    def chunk(n, d):
        return _ml_chunk(n, d, n_chunks)

    kern = functools.partial(_mlstm_kernel, n_heads=n_heads)
    return pl.pallas_call(
        kern,
        grid=(B, 2, n_chunks),
        in_specs=[pl.BlockSpec((1, CHUNK, qk_w), lambda b, d, n: (b, chunk(n, d), qk_blk)),
                  pl.BlockSpec((1, CHUNK, v_w), lambda b, d, n: (b, chunk(n, d), v_blk)),
                  pl.BlockSpec((1, CHUNK, LANES), lambda b, d, n: (b, chunk(n, d), d)),
                  pl.BlockSpec((1, 8, CHUNK), lambda b, d, n: (b, d, chunk(n, d))),
                  pl.BlockSpec((1, 1, LANES), lambda b, d, n: (d, 0, 0)),
                  pl.BlockSpec((1, 8, LANES), lambda b, d, n: (d, 0, 0))],
        out_specs=pl.BlockSpec((1, 1, CHUNK, v_w), lambda b, d, n: (d, b, chunk(n, d), 0)),
        out_shape=jax.ShapeDtypeStruct((2, B, T, v_w), BF16),
        scratch_shapes=[pltpu.VMEM((n_heads, ML_QK_DIM, ML_V_DIM + LANES), F32),
                        pltpu.VMEM((n_heads, 8, LANES), F32)],
        compiler_params=_params(3),
        name="mlstm",
    )(z, z, gcol, grow, gbc, gbr)


def _gated_residual(x, y, post_g, gate):
    var = jnp.mean(y * y, axis=-1, keepdims=True)
    return x + gate * (y * lax.rsqrt(var + EPS) * post_g)


def _merge_kernel(na_ref, hf_ref, hb_ref, mo_ref, gn_ref, gm_ref, x_ref, mod_ref, ng_ref, pg_ref,
                  wna_ref, wml_ref, wo_ref, o_ref, *, n_heads):
    hs = hf_ref[0, 0].astype(F32) + hb_ref[0, 0].astype(F32)
    parts = []
    for hh in range(n_heads):
        hv = hs[:, hh * ML_V_DIM:(hh + 1) * ML_V_DIM]
        var = jnp.mean(hv * hv, axis=-1, keepdims=True)
        parts.append(hv * lax.rsqrt(var + EPS))
    ml = jnp.concatenate(parts, axis=1) * ng_ref[...] * mo_ref[0].astype(F32)
    y = (gn_ref[0].astype(F32) * _dot(na_ref[0], wna_ref[...])
         + gm_ref[0].astype(F32) * _dot(ml.astype(BF16), wml_ref[...]))
    yo = _dot(y.astype(BF16), wo_ref[...])
    o_ref[0] = _gated_residual(x_ref[0], yo, pg_ref[...], mod_ref[0, 0, 2:3, :])


def _merge(na_o, ml_h, z, xc, mods, ng, pg, wna, wml, wo, *, n_heads, mo_col, gn_col, gm_col):
    B, T, D = xc.shape
    na_w = na_o.shape[2]
    v_w = ml_h.shape[3]
    nt = T // CHUNK
    kern = functools.partial(_merge_kernel, n_heads=n_heads)
    const = dict(pipeline_mode=pl.Buffered(1))
    return pl.pallas_call(
        kern,
        grid=(B, nt),
        in_specs=[pl.BlockSpec((1, CHUNK, na_w), lambda b, i: (b, i, 0)),
                  pl.BlockSpec((1, 1, CHUNK, v_w), lambda b, i: (0, b, i, 0)),
                  pl.BlockSpec((1, 1, CHUNK, v_w), lambda b, i: (1, b, i, 0)),
                  pl.BlockSpec((1, CHUNK, v_w), lambda b, i: (b, i, mo_col // v_w)),
                  pl.BlockSpec((1, CHUNK, D), lambda b, i: (b, i, gn_col // D)),
                  pl.BlockSpec((1, CHUNK, D), lambda b, i: (b, i, gm_col // D)),
                  pl.BlockSpec((1, CHUNK, D), lambda b, i: (b, i, 0)),
                  pl.BlockSpec((1, 1, 6, D), lambda b, i: (b, i // (nt - 1), 0, 0)),
                  pl.BlockSpec((1, v_w), lambda b, i: (0, 0)),
                  pl.BlockSpec((1, D), lambda b, i: (0, 0)),
                  pl.BlockSpec((na_w, D), lambda b, i: (0, 0), **const),
                  pl.BlockSpec((v_w, D), lambda b, i: (0, 0), **const),
                  pl.BlockSpec((D, D), lambda b, i: (0, 0), **const)],
        out_specs=pl.BlockSpec((1, CHUNK, D), lambda b, i: (b, i, 0)),
        out_shape=jax.ShapeDtypeStruct((B, T, D), F32),
        compiler_params=_params(2),
        name="merge_out_proj",
    )(na_o, ml_h, ml_h, z, z, z, xc, mods, ng, pg, wna, wml, wo)


def _ffn_up_kernel(x_ref, xp_ref, xn_ref, mod_ref, g_ref, wg_ref, wv_ref, cw_ref, cb_ref,
                   o_ref, h_scr, u_scr, *, n_lat, n_tot, tm, tn, sub):
    i = pl.program_id(1)
    j = pl.program_id(2)
    H = BF16_ROWS

    @pl.when(j == 0)
    def _():
        _fill_h(h_scr, x_ref, xp_ref, xn_ref, mod_ref, g_ref[...], i,
                tm=tm, n_lat=n_lat, n_tiles=n_tot // tm, which=3)

    is_start, is_end = _seq_edges(i, tm, n_lat, n_tot)
    for sc, c0 in enumerate(range(0, tn, sub)):
        width = min(sub, tn - c0)
        cols = slice(c0, c0 + width)
        u = u_scr.at[sc]
        u[:, pl.ds(0, width)] = _dot(h_scr[...], wg_ref[:, cols])
        val = _dot(h_scr[pl.ds(H, tm), :], wv_ref[:, cols])
        y = _conv3(u, width, cw_ref[:, cols], cb_ref[:, cols], is_start, is_end, tm)
        o_ref[0, :, cols] = (y * _sigmoid(y) * val).astype(BF16)


def _ffn_up(xc, mods, g, w_up, cw, cb, *, n_lat, tm, tn):
    B, T, D = xc.shape
    dff = w_up.shape[1] // 2
    nj = dff // tn
    sub = min(tn, MXU_COLS)
    kern = functools.partial(_ffn_up_kernel, n_lat=n_lat, n_tot=T, tm=tm, tn=tn, sub=sub)
    return pl.pallas_call(
        kern,
        grid=(B, T // tm, nj),
        in_specs=[pl.BlockSpec((1, tm, D), lambda b, i, j: (b, i, 0))] + _halo_specs(tm, T, D) + [
                  pl.BlockSpec((1, 2, 6, D), lambda b, i, j: (b, 0, 0, 0)),
                  pl.BlockSpec((1, D), lambda b, i, j: (0, 0)),
                  pl.BlockSpec((D, tn), lambda b, i, j: (0, j)),
                  pl.BlockSpec((D, tn), lambda b, i, j: (0, nj + j)),
                  pl.BlockSpec((3, tn), lambda b, i, j: (0, j)),
                  pl.BlockSpec((1, tn), lambda b, i, j: (0, j))],
        out_specs=pl.BlockSpec((1, tm, tn), lambda b, i, j: (b, i, j)),
        out_shape=jax.ShapeDtypeStruct((B, T, dff), BF16),
        scratch_shapes=[pltpu.VMEM((tm + 2 * BF16_ROWS, D), BF16),
                        pltpu.VMEM((pl.cdiv(tn, sub), tm + 2 * BF16_ROWS, sub), F32)],
        compiler_params=_params(3),
        name="ffn_up",
    )(xc, xc, xc, mods, g, w_up, w_up, cw, cb)


def _ffn_down_kernel(a_ref, x_ref, mod_ref, pg_ref, w_ref, o_ref):
    y = _dot(a_ref[0], w_ref[...])
    o_ref[0] = _gated_residual(x_ref[0], y, pg_ref[...], mod_ref[0, 0, 5:6, :])


def _ffn_down(act, xc, mods, pg, w_down):
    B, T, D = xc.shape
    dff = act.shape[2]
    nt = T // CHUNK
    return pl.pallas_call(
        _ffn_down_kernel,
        grid=(B, nt),
        in_specs=[pl.BlockSpec((1, CHUNK, dff), lambda b, i: (b, i, 0)),
                  pl.BlockSpec((1, CHUNK, D), lambda b, i: (b, i, 0)),
                  pl.BlockSpec((1, 1, 6, D), lambda b, i: (b, i // (nt - 1), 0, 0)),
                  pl.BlockSpec((1, D), lambda b, i: (0, 0)),
                  pl.BlockSpec((dff, D), lambda b, i: (0, 0), pipeline_mode=pl.Buffered(1))],
        out_specs=pl.BlockSpec((1, CHUNK, D), lambda b, i: (b, i, 0)),
        out_shape=jax.ShapeDtypeStruct((B, T, D), F32),
        compiler_params=_params(2),
        name="ffn_down",
    )(act, xc, mods, pg, w_down)


def _rope_tables(n_lat, n_ctx):
    t = np.arange(n_lat)
    row = (t // GRID_W).astype(np.float32)
    col = (t % GRID_W).astype(np.float32)
    n_freq = NA_HEAD_DIM // 4
    inv_freq = jnp.asarray(ROPE_THETA, F32) ** (-jnp.arange(n_freq, dtype=F32) / n_freq)
    ang = jnp.concatenate([jnp.asarray(row)[:, None] * inv_freq, jnp.asarray(col)[:, None] * inv_freq], axis=-1)
    cos, sin = jnp.cos(ang), jnp.sin(ang)
    cs = jnp.concatenate([cos, cos], axis=1)
    sn = jnp.concatenate([-sin, sin], axis=1)
    cs = jnp.concatenate([cs, jnp.ones((n_ctx, NA_HEAD_DIM), F32)], axis=0)
    sn = jnp.concatenate([sn, jnp.zeros((n_ctx, NA_HEAD_DIM), F32)], axis=0)
    return cs, sn


def kernel(x, c, ctx, c_ctx, w_mod, b_mod, norm_mix_pre, norm_mix_post, norm_ffn_pre, norm_ffn_post, w_in, na_rpb, ml_conv_w, ml_conv_b, ml_igate_b, ml_fgate_b, ml_norm_g, w_na_proj, w_ml_proj, w_out, w_up, ffn_conv_w, ffn_conv_b, w_down):
    B, S, D = x.shape
    CL = ctx.shape[1]
    T = S + CL
    L = w_mod.shape[0]
    na_w = w_na_proj.shape[1]
    v_w = w_ml_proj.shape[1]
    n_na_heads = na_w // NA_HEAD_DIM
    n_ml_heads = v_w // ML_V_DIM
    qk_w = 2 * n_ml_heads * ML_QK_DIM
    n_gate = 2 * n_ml_heads
    dff = w_down.shape[1]
    assert CL == CHUNK and S % CHUNK == 0 and S % GRID_W == 0 and S // GRID_W >= NA_KEY_ROWS
    assert n_ml_heads <= 4 and B + 1 <= MOD_ROWS
    gate_col = 3 * na_w + qk_w + 2 * v_w
    assert w_in.shape[2] == gate_col + 2 * n_gate + 2 * D

    qk_col = 3 * na_w
    v_col = qk_col + qk_w
    mo_col = v_col + v_w
    gn_col = mo_col + v_w
    gm_col = gn_col + D
    tn = _pick_tile(gm_col + D, (2048, 1024, 512, 256))
    assert all(edge % MXU_COLS == 0 for edge in (na_w, qk_col, v_col, mo_col))
    assert qk_col % qk_w == 0 and v_col % v_w == 0 and mo_col % v_w == 0 and gn_col % D == 0
    tm = _pick_tile(T, (768, 256))
    assert CL <= tm and (S - (T // tm - 1) * tm) % BF16_ROWS == 0
    tn_ff = _pick_tile(dff, (512, 256))

    crow = jnp.concatenate([c, c_ctx[None], jnp.zeros((MOD_ROWS - B - 1, D), F32)], axis=0)
    mod_all = _modulation(crow, w_mod, b_mod).reshape(L, MOD_ROWS, 6, D)
    mods_all = jnp.stack([mod_all[:, :B], jnp.broadcast_to(mod_all[:, B:B + 1], (L, B, 6, D))], axis=2)

    cs, sn = _rope_tables(S, CL)
    tabs = _na_bias_tables(na_rpb, S // GRID_W)

    gb = jnp.concatenate([ml_igate_b, ml_fgate_b], axis=2)
    gbc_all = jnp.pad(gb, ((0, 0), (0, 0), (0, LANES - n_gate)))[:, :, None, :]
    gbr_all = jnp.broadcast_to(jnp.pad(gb, ((0, 0), (0, 0), (0, 8 - n_gate)))[..., None], (L, 2, 8, LANES))
    qs = jnp.concatenate([jnp.full((1, qk_w // 2), ML_QK_DIM ** -0.5, F32), jnp.ones((1, qk_w // 2), F32)], axis=1)

    xc = jnp.concatenate([x, ctx], axis=1)
    for l in range(L):
        mods = mods_all[l]
        wl = w_in[l]
        w_main = jnp.concatenate([wl[:, :gate_col], wl[:, gate_col + 2 * n_gate:]], axis=1).astype(BF16)
        ig = wl[:, gate_col:gate_col + n_gate].reshape(D, 2, n_ml_heads)
        fg = wl[:, gate_col + n_gate:gate_col + 2 * n_gate].reshape(D, 2, n_ml_heads)
        gcols = jnp.concatenate([ig, fg], axis=2)
        wg = jnp.pad(gcols, ((0, 0), (0, 0), (0, LANES - n_gate))).reshape(D, 2 * LANES).astype(BF16)
        wgt = jnp.pad(gcols, ((0, 0), (0, 0), (0, 8 - n_gate))).reshape(D, 16).T.astype(BF16)

        z, gcol, grow = _in_projection(
            xc, mods, norm_mix_pre[l][None], cs, sn, w_main, wg, wgt, ml_conv_w[l], ml_conv_b[l][None], qs,
            n_lat=S, tm=tm, tn=tn, na_w=na_w, qk_col=qk_col, v_col=v_col, mo_col=mo_col)
        na_o = _neighbourhood_attention(z, tabs[l], n_lat=S, n_heads=n_na_heads, na_w=na_w)
        ml_h = _mlstm(z, gcol, grow, gbc_all[l], gbr_all[l], n_heads=n_ml_heads, qk_col=qk_col, v_col=v_col)
        xc = _merge(na_o, ml_h, z, xc, mods, ml_norm_g[l][None], norm_mix_post[l][None],
                    w_na_proj[l].astype(BF16), w_ml_proj[l].astype(BF16), w_out[l].astype(BF16),
                    n_heads=n_ml_heads, mo_col=mo_col, gn_col=gn_col, gm_col=gm_col)
        act = _ffn_up(xc, mods, norm_ffn_pre[l][None], w_up[l].astype(BF16), ffn_conv_w[l], ffn_conv_b[l][None],
                      n_lat=S, tm=tm, tn=tn_ff)
        xc = _ffn_down(act, xc, mods, norm_ffn_post[l][None], w_down[l].astype(BF16))
    return xc[:, :S]
```

```python
import functools

import numpy as np
import jax
import jax.numpy as jnp
from jax import lax
from jax.experimental import pallas as pl
from jax.experimental.pallas import tpu as pltpu

GRID_W = 64
NA_HEAD_DIM = 128
NA_WIN_H = 8
NA_WIN_W = 16
ROPE_THETA = 10000.0
ML_QK_DIM = 128
ML_V_DIM = 256
EPS = 1e-6
M_INIT = -1e30
NEG_BIG = -1e30

LANES = 128
MXU_COLS = 256
BF16_ROWS = 16
MOD_ROWS = 16
CHUNK = 256
NA_Q_ROWS = CHUNK // GRID_W
NA_KEY_ROWS = 12
NA_MASKED = 2 * NA_WIN_H - 1
VMEM_LIMIT = 56 * 1024 * 1024

F32 = jnp.float32
BF16 = jnp.bfloat16


def _dot(a, b):
    return jnp.dot(a, b, preferred_element_type=F32)


def _dot_nt(a, b):
    return lax.dot_general(a, b, (((1,), (1,)), ((), ())), preferred_element_type=F32)


def _dot_tn(a, b):
    return lax.dot_general(a, b, (((0,), (0,)), ((), ())), preferred_element_type=F32)


def _sigmoid(x):
    return 0.5 * jnp.tanh(0.5 * x) + 0.5


def _log_sigmoid(x):
    return jnp.minimum(x, 0.0) - jnp.log(1.0 + jnp.exp(-jnp.abs(x)))


def _split3(x):
    hi = x.astype(BF16)
    r1 = x - hi.astype(F32)
    mid = r1.astype(BF16)
    lo = (r1 - mid.astype(F32)).astype(BF16)
    return hi, mid, lo


def _params(n_grid):
    return pltpu.CompilerParams(dimension_semantics=("arbitrary",) * n_grid,
                                vmem_limit_bytes=VMEM_LIMIT)


def _pick_tile(n, candidates):
    for c in candidates:
        if n % c == 0:
            return c
    raise ValueError(f"no tile for {n} among {candidates}")


def _norm_mod(x, g, mod_ref, which):
    var = jnp.mean(x * x, axis=-1, keepdims=True)
    y = x * lax.rsqrt(var + EPS) * g
    return (y * (1.0 + mod_ref[0, 0, 0, which + 1:which + 2, :]) + mod_ref[0, 0, 0, which:which + 1, :]).astype(BF16)


def _gated_residual(x, y, post_g, gate):
    var = jnp.mean(y * y, axis=-1, keepdims=True)
    return x + gate * (y * lax.rsqrt(var + EPS) * post_g)


def _mod_kernel(c_ref, w_ref, b_ref, o_ref):
    c = c_ref[...]
    s = c * _sigmoid(c)
    hi, mid, lo = _split3(s)
    whi, wmid, wlo = _split3(w_ref[0])
    acc = _dot(hi, whi) + (_dot(hi, wmid) + _dot(mid, whi)) + (_dot(hi, wlo) + _dot(mid, wmid) + _dot(lo, whi))
    o_ref[0] = acc + b_ref[0]


def _modulation(crow, w_mod, b_mod):
    L, D, N = w_mod.shape
    tn = _pick_tile(N, (1024, 512, 256, 128))
    return pl.pallas_call(
        _mod_kernel,
        grid=(L, N // tn),
        in_specs=[pl.BlockSpec((MOD_ROWS, D), lambda l, j: (0, 0)),
                  pl.BlockSpec((1, D, tn), lambda l, j: (l, 0, j)),
                  pl.BlockSpec((1, 1, tn), lambda l, j: (l, 0, j))],
        out_specs=pl.BlockSpec((1, MOD_ROWS, tn), lambda l, j: (l, 0, j)),
        out_shape=jax.ShapeDtypeStruct((L, MOD_ROWS, N), F32),
        compiler_params=_params(2),
        name="adaln_mod",
    )(crow, w_mod, b_mod.reshape(L, 1, N))


def _prenorm_kernel(x_ref, mod_ref, g_ref, h_ref):
    h_ref[0] = _norm_mod(x_ref[0], g_ref[0], mod_ref, 0)


def _prenorm(xc, mods_all, g_all):
    B, T, D = xc.shape
    nt = T // CHUNK
    return pl.pallas_call(
        _prenorm_kernel,
        grid=(B, nt),
        in_specs=[pl.BlockSpec((1, CHUNK, D), lambda b, i: (b, i, 0)),
                  pl.BlockSpec((1, 1, 1, 6, D), lambda b, i: (0, b, i // (nt - 1), 0, 0)),
                  pl.BlockSpec((1, 1, D), lambda b, i: (0, 0, 0))],
        out_specs=pl.BlockSpec((1, CHUNK, D), lambda b, i: (b, i, 0)),
        out_shape=jax.ShapeDtypeStruct((B, T, D), BF16),
        compiler_params=_params(2),
        name="prenorm",
    )(xc, mods_all, g_all)


def _fill_h(h_scr, h_ref, hp_ref, hn_ref, tm):
    H = BF16_ROWS
    h_scr[pl.ds(0, H), :] = hp_ref[0]
    h_scr[pl.ds(H, tm), :] = h_ref[0]
    h_scr[pl.ds(H + tm, H), :] = hn_ref[0]


def _seq_edges(i, tm, n_lat, n_tot):
    row = i * tm + lax.broadcasted_iota(jnp.int32, (tm, 1), 0)
    is_start = jnp.logical_or(row == 0, row == n_lat)
    is_end = jnp.logical_or(row == n_lat - 1, row == n_tot - 1)
    return is_start, is_end


def _conv3(u_ref, width, cw, cb, is_start, is_end, tm):
    H = BF16_ROWS
    u_prev = jnp.where(is_start, 0.0, u_ref[pl.ds(H - 1, tm), pl.ds(0, width)])
    u_next = jnp.where(is_end, 0.0, u_ref[pl.ds(H + 1, tm), pl.ds(0, width)])
    return cw[0:1, :] * u_prev + cw[1:2, :] * u_ref[pl.ds(H, tm), pl.ds(0, width)] + cw[2:3, :] * u_next + cb


def _h_specs(tm, T, D):
    hb = tm // BF16_ROWS
    n_hblk = T // BF16_ROWS
    return [pl.BlockSpec((1, tm, D), lambda b, i, j: (b, i, 0)),
            pl.BlockSpec((1, BF16_ROWS, D), lambda b, i, j: (b, jnp.maximum(i * hb - 1, 0), 0)),
            pl.BlockSpec((1, BF16_ROWS, D), lambda b, i, j: (b, jnp.minimum((i + 1) * hb, n_hblk - 1), 0))]


def _inproj_kernel(h_ref, hp_ref, hn_ref, cs_ref, sn_ref, w_ref, wg_ref, cw_ref, cb_ref, qs_ref,
                   z_ref, gcol_ref, h_scr, u_scr, *,
                   n_lat, n_tot, tm, tn, n_col, sub, na_w, qk_col, v_col, mo_col, q_scale):
    i = pl.program_id(1)
    j = pl.program_id(2)

    @pl.when(j == 0)
    def _():
        _fill_h(h_scr, h_ref, hp_ref, hn_ref, tm)
        gcol_ref[0] = _dot(h_ref[0], wg_ref[0])

    def col_tile(jj):
        is_start, is_end = _seq_edges(i, tm, n_lat, n_tot)
        for sc in range(tn // sub):
            col = jj * tn + sc * sub
            cols = slice(sc * sub, (sc + 1) * sub)
            if qk_col <= col < v_col:
                u = u_scr.at[(col - qk_col) // sub]
                u[...] = _dot(h_scr[...], w_ref[0, :, cols])
                cc = slice(col - qk_col, col - qk_col + sub)
                y = _conv3(u, sub, cw_ref[0, :, cc], cb_ref[0, :, cc], is_start, is_end, tm)
                z_ref[0, :, cols] = (y * _sigmoid(y) * qs_ref[:, cc]).astype(BF16)
                continue
            acc = _dot(h_ref[0], w_ref[0, :, cols])
            if col < 2 * na_w:
                for c in range(sub // LANES):
                    chunk = acc[:, c * LANES:(c + 1) * LANES]
                    out = chunk * cs_ref[...] + pltpu.roll(chunk, LANES // 2, axis=1) * sn_ref[...]
                    if col < na_w:
                        out = out * q_scale
                    z_ref[0, :, sc * sub + c * LANES:sc * sub + (c + 1) * LANES] = out.astype(BF16)
            elif col < mo_col:
                z_ref[0, :, cols] = acc.astype(BF16)
            else:
                z_ref[0, :, cols] = _sigmoid(acc).astype(BF16)

    for jj in range(n_col):
        pl.when(j == jj)(functools.partial(col_tile, jj))


def _in_projection(h, cs, sn, w_all, wg_all, cw_all, cb_all, qs, layer, *, n_lat, tm, tn, na_w, qk_col, v_col, mo_col):
    B, T, D = h.shape
    NZ = w_all.shape[2]
    sub = min(tn, MXU_COLS)
    qk_w = v_col - qk_col
    kern = functools.partial(_inproj_kernel, n_lat=n_lat, n_tot=T, tm=tm, tn=tn, n_col=NZ // tn, sub=sub, na_w=na_w,
                             qk_col=qk_col, v_col=v_col, mo_col=mo_col, q_scale=NA_HEAD_DIM ** -0.5)
    return pl.pallas_call(
        kern,
        grid=(B, T // tm, NZ // tn),
        in_specs=_h_specs(tm, T, D) + [
                  pl.BlockSpec((tm, LANES), lambda b, i, j: (i, 0)),
                  pl.BlockSpec((tm, LANES), lambda b, i, j: (i, 0)),
                  pl.BlockSpec((1, D, tn), lambda b, i, j: (layer, 0, j)),
                  pl.BlockSpec((1, D, 2 * LANES), lambda b, i, j: (layer, 0, 0)),
                  pl.BlockSpec((1, 3, qk_w), lambda b, i, j: (layer, 0, 0)),
                  pl.BlockSpec((1, 1, qk_w), lambda b, i, j: (layer, 0, 0)),
                  pl.BlockSpec((1, qk_w), lambda b, i, j: (0, 0))],
        out_specs=[pl.BlockSpec((1, tm, tn), lambda b, i, j: (b, i, j)),
                   pl.BlockSpec((1, tm, 2 * LANES), lambda b, i, j: (b, i, 0))],
        out_shape=[jax.ShapeDtypeStruct((B, T, NZ), BF16),
                   jax.ShapeDtypeStruct((B, T, 2 * LANES), F32)],
        scratch_shapes=[pltpu.VMEM((tm + 2 * BF16_ROWS, D), BF16),
                        pltpu.VMEM((qk_w // sub, tm + 2 * BF16_ROWS, sub), F32)],
        compiler_params=_params(3),
        name="in_proj",
    )(h, h, h, cs, sn, w_all, wg_all, cw_all, cb_all, qs)


def _na_layout(n_rows):
    n_tiles = n_rows // NA_Q_ROWS
    pairs, plan = [], []
    for gt in (0, 1, n_tiles - 1):
        ws = int(np.clip(NA_Q_ROWS * gt - NA_WIN_H // 2, 0, n_rows - NA_KEY_ROWS))
        per_rq = []
        for rq in range(NA_Q_ROWS):
            r = NA_Q_ROWS * gt + rq
            row_start = int(np.clip(r - NA_WIN_H // 2, 0, n_rows - NA_WIN_H))
            per_pair = []
            for p in range(NA_KEY_ROWS // 2):
                ds = []
                for key_row in (ws + 2 * p, ws + 2 * p + 1):
                    ok = row_start <= key_row < row_start + NA_WIN_H
                    ds.append(key_row - r + NA_WIN_H - 1 if ok else NA_MASKED)
                ds = tuple(ds)
                if ds == (NA_MASKED, NA_MASKED):
                    per_pair.append(None)
                    continue
                if ds not in pairs:
                    pairs.append(ds)
                per_pair.append(pairs.index(ds))
            per_rq.append(per_pair)
        plan.append(per_rq)
    return pairs, plan


def _na_bias_pairs(rpb, pairs):
    L, H = rpb.shape[:2]
    kw = NA_WIN_W
    cq = np.arange(GRID_W)[:, None]
    ck = np.arange(GRID_W)[None, :]
    dc = np.clip(ck - cq, 1 - kw, kw - 1) + kw - 1
    col_start = np.clip(cq - kw // 2, 0, GRID_W - kw)
    ok_c = (ck >= col_start) & (ck < col_start + kw)
    onehot = (dc[..., None] == np.arange(2 * kw - 1)).astype(np.float32)
    toe = jnp.einsum("lhdj,qkj->lhdqk", rpb, jnp.asarray(onehot), precision=lax.Precision.HIGHEST)
    toe = jnp.where(jnp.asarray(ok_c), toe, NEG_BIG)
    toe = jnp.concatenate([toe, jnp.full((L, H, 1, GRID_W, GRID_W), NEG_BIG, F32)], axis=2)
    return jnp.stack([jnp.concatenate([toe[:, :, da], toe[:, :, db]], axis=-1) for da, db in pairs], axis=2)


def _na_kernel(q_ref, k_ref, v_ref, pair_ref, o_ref, *, n_lat, n_lat_tiles, win, hpb, plan):
    g = pl.program_id(2)
    n_pairs = NA_KEY_ROWS // 2

    def latent(tile_plan):
        start = pl.multiple_of(jnp.clip((g - 1) * CHUNK, 0, n_lat - win), CHUNK)
        for hh in range(hpb):
            hs = slice(hh * LANES, (hh + 1) * LANES)
            q = q_ref[0, :, hs]
            vc = v_ref[0, pl.ds(n_lat, CHUNK), hs]
            vl = v_ref[0, pl.ds(start, win), hs]
            s_ctx = _dot_nt(q, k_ref[0, pl.ds(n_lat, CHUNK), hs])
            s_loc = _dot_nt(q, k_ref[0, pl.ds(start, win), hs])
            e_rows, ec_rows, inv_rows = [], [], []
            for rq in range(NA_Q_ROWS):
                rs = slice(rq * GRID_W, (rq + 1) * GRID_W)
                blocks = [None if ci is None else
                          s_loc[rs, p * LANES:(p + 1) * LANES] + pair_ref[0, hh, ci]
                          for p, ci in enumerate(tile_plan[rq])]
                live = [blk for blk in blocks if blk is not None]
                sc = s_ctx[rs]
                top = functools.reduce(jnp.maximum, live + [sc[:, :LANES], sc[:, LANES:]])
                m = jnp.max(top, axis=1, keepdims=True)
                e_blocks = [jnp.zeros((GRID_W, LANES), F32) if blk is None else jnp.exp(blk - m) for blk in blocks]
                e_ctx = jnp.exp(sc - m)
                tot = functools.reduce(jnp.add, [e for e, blk in zip(e_blocks, blocks) if blk is not None]
                                       + [e_ctx[:, :LANES], e_ctx[:, LANES:]])
                inv_rows.append(1.0 / jnp.sum(tot, axis=1, keepdims=True))
                e_rows.append(jnp.concatenate(e_blocks, axis=1).astype(BF16))
                ec_rows.append(e_ctx.astype(BF16))
            o = _dot(jnp.concatenate(e_rows, axis=0), vl) + _dot(jnp.concatenate(ec_rows, axis=0), vc)
            o_ref[0, :, hs] = (o * jnp.concatenate(inv_rows, axis=0)).astype(BF16)

    pl.when(g == 0)(functools.partial(latent, plan[0]))
    pl.when(jnp.logical_and(g > 0, g < n_lat_tiles - 1))(functools.partial(latent, plan[1]))
    pl.when(g == n_lat_tiles - 1)(functools.partial(latent, plan[2]))

    @pl.when(g == n_lat_tiles)
    def _():
        for hh in range(hpb):
            hs = slice(hh * LANES, (hh + 1) * LANES)
            vc = v_ref[0, pl.ds(n_lat, CHUNK), hs]
            s_ctx = _dot_nt(q_ref[0, :, hs], k_ref[0, pl.ds(n_lat, CHUNK), hs])
            e_ctx = jnp.exp(s_ctx - jnp.max(s_ctx, axis=1, keepdims=True))
            denom = jnp.sum(e_ctx, axis=1, keepdims=True)
            o_ref[0, :, hs] = (_dot(e_ctx.astype(BF16), vc) / denom).astype(BF16)


def _neighbourhood_attention(z, pair_all, plan, layer, *, n_lat, n_heads, na_w):
    B, T, _ = z.shape
    n_lat_tiles = n_lat // CHUNK
    win = NA_KEY_ROWS * GRID_W
    hpb = _pick_tile(n_heads, (4, 2, 1))
    n_hg = n_heads // hpb
    n_pair_kinds = pair_all.shape[2]
    kern = functools.partial(_na_kernel, n_lat=n_lat, n_lat_tiles=n_lat_tiles, win=win, hpb=hpb, plan=plan)
    return pl.pallas_call(
        kern,
        grid=(B, n_hg, n_lat_tiles + 1),
        in_specs=[pl.BlockSpec((1, CHUNK, hpb * LANES), lambda b, h, g: (b, g, h)),
                  pl.BlockSpec((1, T, hpb * LANES), lambda b, h, g: (b, 0, n_hg + h)),
                  pl.BlockSpec((1, T, hpb * LANES), lambda b, h, g: (b, 0, 2 * n_hg + h)),
                  pl.BlockSpec((1, hpb, n_pair_kinds, GRID_W, LANES), lambda b, h, g: (layer, h, 0, 0, 0))],
        out_specs=pl.BlockSpec((1, CHUNK, hpb * LANES), lambda b, h, g: (b, g, h)),
        out_shape=jax.ShapeDtypeStruct((B, T, na_w), BF16),
        compiler_params=_params(3),
        name="na_attn",
    )(z, z, z, pair_all)


def _ml_chunk(n, d, n_chunks):
    return jnp.where(n == 0, n_chunks - 1, jnp.where(d == 0, n - 1, n_chunks - 1 - n))


def _mlstm_kernel(qk_ref, v_ref, gcol_ref, gbc_ref, o_ref, c_scr, m_scr, *, n_heads):
    d = pl.program_id(1)
    n = pl.program_id(2)
    L = CHUNK

    @pl.when(n == 0)
    def _():
        c_scr[...] = jnp.zeros_like(c_scr)
        m_scr[...] = jnp.full_like(m_scr, M_INIT)

    ga = gcol_ref[0] + gbc_ref[0, 0]
    lf_c = _log_sigmoid(ga)
    gr = ga.T[0:8, :]
    lf_r = lf_c.T[0:8, :]

    sgn = 1 - 2 * d
    ti = lax.broadcasted_iota(jnp.int32, (L, L), 0)
    si = lax.broadcasted_iota(jnp.int32, (L, L), 1)
    mask = ((si - ti) * sgn) <= 0
    maskb = mask.astype(F32).astype(BF16)
    hi, mid, lo = _split3(lf_c)
    b_c = _dot(maskb, hi) + _dot(maskb, mid) + _dot(maskb, lo)
    hi, mid, lo = _split3(jnp.concatenate([lf_r, jnp.zeros_like(lf_r)], axis=0))
    b_r = _dot_nt(hi, maskb) + _dot_nt(mid, maskb) + _dot_nt(lo, maskb)

    ones = jnp.ones((L, LANES), BF16)
    for hh in range(n_heads):
        q = qk_ref[0, :, hh * ML_QK_DIM:(hh + 1) * ML_QK_DIM]
        k = qk_ref[0, :, (n_heads + hh) * ML_QK_DIM:(n_heads + hh + 1) * ML_QK_DIM]
        vext = jnp.concatenate([v_ref[0, :, hh * ML_V_DIM:(hh + 1) * ML_V_DIM], ones], axis=1)
        li_c = ga[:, hh:hh + 1]
        b_col = b_c[:, n_heads + hh:n_heads + hh + 1]
        li_r = gr[hh:hh + 1, :]
        b_row = b_r[n_heads + hh:n_heads + hh + 1, :]
        g_tot = jnp.sum(lf_r[n_heads + hh:n_heads + hh + 1, :], axis=1, keepdims=True)
        m_prev = m_scr[hh, 0:1, 0:1]
        c_prev = c_scr[hh]

        dm = jnp.where(mask, b_col + (li_r - b_row), NEG_BIG)
        inter = b_col + m_prev
        m_t = jnp.maximum(inter, jnp.max(dm, axis=1, keepdims=True))
        w_inter = jnp.exp(inter - m_t)
        p = jnp.exp(dm - m_t)
        s = (_dot_nt(q, k) * p).astype(BF16)
        r = _dot(s, vext) + w_inter * _dot(q, c_prev.astype(BF16))
        den = jnp.maximum(jnp.abs(r[:, ML_V_DIM:ML_V_DIM + 1]), jnp.exp(-m_t))
        o_ref[0, 0, :, hh * ML_V_DIM:(hh + 1) * ML_V_DIM] = (r[:, :ML_V_DIM] / den).astype(BF16)

        a_col = g_tot + li_c - b_col
        ma = jnp.max(a_col, axis=0, keepdims=True)
        wk = (jnp.exp(a_col - ma) * k.astype(F32)).astype(BF16)
        c_loc = _dot_tn(wk, vext)
        m_new = jnp.maximum(g_tot + m_prev, ma)
        dec = jnp.exp(g_tot + m_prev - m_new)
        inc = jnp.exp(ma - m_new)
        c_scr[hh] = dec * c_prev + inc * c_loc
        m_scr[hh] = jnp.broadcast_to(m_new, m_scr.shape[1:])


def _mlstm(z, gcol, gbc_all, layer, *, n_heads, qk_col, v_col):
    B, T, _ = z.shape
    n_chunks = T // CHUNK
    qk_w = 2 * n_heads * ML_QK_DIM
    v_w = n_heads * ML_V_DIM
    qk_blk = qk_col // qk_w
    v_blk = v_col // v_w

    def chunk(n, d):
        return _ml_chunk(n, d, n_chunks)

    kern = functools.partial(_mlstm_kernel, n_heads=n_heads)
    return pl.pallas_call(
        kern,
        grid=(B, 2, n_chunks),
        in_specs=[pl.BlockSpec((1, CHUNK, qk_w), lambda b, d, n: (b, chunk(n, d), qk_blk)),
                  pl.BlockSpec((1, CHUNK, v_w), lambda b, d, n: (b, chunk(n, d), v_blk)),
                  pl.BlockSpec((1, CHUNK, LANES), lambda b, d, n: (b, chunk(n, d), d)),
                  pl.BlockSpec((1, 1, 1, LANES), lambda b, d, n: (layer, d, 0, 0))],
        out_specs=pl.BlockSpec((1, 1, CHUNK, v_w), lambda b, d, n: (d, b, chunk(n, d), 0)),
        out_shape=jax.ShapeDtypeStruct((2, B, T, v_w), BF16),
        scratch_shapes=[pltpu.VMEM((n_heads, ML_QK_DIM, ML_V_DIM + LANES), F32),
                        pltpu.VMEM((n_heads, 8, LANES), F32)],
        compiler_params=_params(3),
        name="mlstm",
    )(z, z, gcol, gbc_all)


def _merge_kernel(na_ref, hf_ref, hb_ref, mo_ref, gn_ref, gm_ref, x_ref, mod_ref, ng_ref, pg_ref, fg_ref,
                  wna_ref, wml_ref, wo_ref, o_ref, h_ref, *, n_heads):
    hs = hf_ref[0, 0].astype(F32) + hb_ref[0, 0].astype(F32)
    parts = []
    for hh in range(n_heads):
        hv = hs[:, hh * ML_V_DIM:(hh + 1) * ML_V_DIM]
        var = jnp.mean(hv * hv, axis=-1, keepdims=True)
        parts.append(hv * lax.rsqrt(var + EPS))
    ml = jnp.concatenate(parts, axis=1) * ng_ref[0] * mo_ref[0].astype(F32)
    y = (gn_ref[0].astype(F32) * _dot(na_ref[0], wna_ref[0])
         + gm_ref[0].astype(F32) * _dot(ml.astype(BF16), wml_ref[0]))
    yo = _dot(y.astype(BF16), wo_ref[0])
    x_new = _gated_residual(x_ref[0], yo, pg_ref[0], mod_ref[0, 0, 0, 2:3, :])
    o_ref[0] = x_new
    h_ref[0] = _norm_mod(x_new, fg_ref[0], mod_ref, 3)


def _merge(na_o, ml_h, z, xc, mods_all, ng_all, pg_all, fg_all, wna_all, wml_all, wo_all, layer, *,
           n_heads, mo_col, gn_col, gm_col):
    B, T, D = xc.shape
    na_w = na_o.shape[2]
    v_w = ml_h.shape[3]
    nt = T // CHUNK
    kern = functools.partial(_merge_kernel, n_heads=n_heads)
    const = dict(pipeline_mode=pl.Buffered(1))
    row = lambda b, i: (b, i, 0)
    vec = lambda b, i: (layer, 0, 0)
    return pl.pallas_call(
        kern,
        grid=(B, nt),
        in_specs=[pl.BlockSpec((1, CHUNK, na_w), row),
                  pl.BlockSpec((1, 1, CHUNK, v_w), lambda b, i: (0, b, i, 0)),
                  pl.BlockSpec((1, 1, CHUNK, v_w), lambda b, i: (1, b, i, 0)),
                  pl.BlockSpec((1, CHUNK, v_w), lambda b, i: (b, i, mo_col // v_w)),
                  pl.BlockSpec((1, CHUNK, D), lambda b, i: (b, i, gn_col // D)),
                  pl.BlockSpec((1, CHUNK, D), lambda b, i: (b, i, gm_col // D)),
                  pl.BlockSpec((1, CHUNK, D), row),
                  pl.BlockSpec((1, 1, 1, 6, D), lambda b, i: (layer, b, i // (nt - 1), 0, 0)),
                  pl.BlockSpec((1, 1, v_w), vec),
                  pl.BlockSpec((1, 1, D), vec),
                  pl.BlockSpec((1, 1, D), vec),
                  pl.BlockSpec((1, na_w, D), vec, **const),
                  pl.BlockSpec((1, v_w, D), vec, **const),
                  pl.BlockSpec((1, D, D), vec, **const)],
        out_specs=[pl.BlockSpec((1, CHUNK, D), row), pl.BlockSpec((1, CHUNK, D), row)],
        out_shape=[jax.ShapeDtypeStruct((B, T, D), F32), jax.ShapeDtypeStruct((B, T, D), BF16)],
        compiler_params=_params(2),
        name="merge_out_proj",
    )(na_o, ml_h, ml_h, z, z, z, xc, mods_all, ng_all, pg_all, fg_all, wna_all, wml_all, wo_all)


def _ffn_up_kernel(h_ref, hp_ref, hn_ref, wg_ref, wv_ref, cw_ref, cb_ref, o_ref, h_scr, u_scr, *,
                   n_lat, n_tot, tm, tn, sub):
    i = pl.program_id(1)
    j = pl.program_id(2)
    H = BF16_ROWS

    @pl.when(j == 0)
    def _():
        _fill_h(h_scr, h_ref, hp_ref, hn_ref, tm)

    is_start, is_end = _seq_edges(i, tm, n_lat, n_tot)
    for sc, c0 in enumerate(range(0, tn, sub)):
        width = min(sub, tn - c0)
        cols = slice(c0, c0 + width)
        u = u_scr.at[sc]
        if width == sub:
            u[...] = _dot(h_scr[...], wg_ref[0, :, cols])
            val = _dot(h_ref[0], wv_ref[0, :, cols])
        else:
            both = _dot(h_scr[...], jnp.concatenate([wg_ref[0, :, cols], wv_ref[0, :, cols]], axis=1))
            u[:, pl.ds(0, width)] = both[:, :width]
            val = both[H:H + tm, width:]
        y = _conv3(u, width, cw_ref[0, :, cols], cb_ref[0, :, cols], is_start, is_end, tm)
        o_ref[0, :, cols] = (y * _sigmoid(y) * val).astype(BF16)


def _ffn_up(h, w_all, cw_all, cb_all, layer, *, n_lat, tm, tn):
    B, T, D = h.shape
    dff = w_all.shape[2] // 2
    nj = dff // tn
    sub = min(tn, MXU_COLS)
    assert 2 * (tn % sub) in (0, sub)
    kern = functools.partial(_ffn_up_kernel, n_lat=n_lat, n_tot=T, tm=tm, tn=tn, sub=sub)
    return pl.pallas_call(
        kern,
        grid=(B, T // tm, nj),
        in_specs=_h_specs(tm, T, D) + [
                  pl.BlockSpec((1, D, tn), lambda b, i, j: (layer, 0, j)),
                  pl.BlockSpec((1, D, tn), lambda b, i, j: (layer, 0, nj + j)),
                  pl.BlockSpec((1, 3, tn), lambda b, i, j: (layer, 0, j)),
                  pl.BlockSpec((1, 1, tn), lambda b, i, j: (layer, 0, j))],
        out_specs=pl.BlockSpec((1, tm, tn), lambda b, i, j: (b, i, j)),
        out_shape=jax.ShapeDtypeStruct((B, T, dff), BF16),
        scratch_shapes=[pltpu.VMEM((tm + 2 * BF16_ROWS, D), BF16),
                        pltpu.VMEM((pl.cdiv(tn, sub), tm + 2 * BF16_ROWS, sub), F32)],
        compiler_params=_params(3),
        name="ffn_up",
    )(h, h, h, w_all, w_all, cw_all, cb_all)


def _ffn_down_kernel(a_ref, x_ref, mod_ref, pg_ref, w_ref, *rest, emit_h):
    y = _dot(a_ref[0], w_ref[0])
    x_new = _gated_residual(x_ref[0], y, pg_ref[0], mod_ref[0, 0, 0, 5:6, :])
    if emit_h:
        modn_ref, gn_ref, o_ref, h_ref = rest
        h_ref[0] = _norm_mod(x_new, gn_ref[0], modn_ref, 0)
    else:
        (o_ref,) = rest
    o_ref[0] = x_new


def _ffn_down(act, xc, mods_all, pg_all, w_all, gpre_all, layer, *, last, n_lat):
    B, T, D = xc.shape
    dff = act.shape[2]
    nt = T // CHUNK
    row = lambda b, i: (b, i, 0)
    mod = lambda lyr: pl.BlockSpec((1, 1, 1, 6, D), lambda b, i: (lyr, b, i // (nt - 1), 0, 0))
    in_specs = [pl.BlockSpec((1, CHUNK, dff), row),
                pl.BlockSpec((1, CHUNK, D), row),
                mod(layer),
                pl.BlockSpec((1, 1, D), lambda b, i: (layer, 0, 0)),
                pl.BlockSpec((1, dff, D), lambda b, i: (layer, 0, 0), pipeline_mode=pl.Buffered(1))]
    args = [act, xc, mods_all, pg_all, w_all]
    if last:
        grid = (B, n_lat // CHUNK)
        out_specs = pl.BlockSpec((1, CHUNK, D), row)
        out_shape = jax.ShapeDtypeStruct((B, n_lat, D), F32)
    else:
        grid = (B, nt)
        in_specs += [mod(layer + 1), pl.BlockSpec((1, 1, D), lambda b, i: (layer + 1, 0, 0))]
        args += [mods_all, gpre_all]
        out_specs = [pl.BlockSpec((1, CHUNK, D), row), pl.BlockSpec((1, CHUNK, D), row)]
        out_shape = [jax.ShapeDtypeStruct((B, T, D), F32), jax.ShapeDtypeStruct((B, T, D), BF16)]
    return pl.pallas_call(
        functools.partial(_ffn_down_kernel, emit_h=not last),
        grid=grid, in_specs=in_specs, out_specs=out_specs, out_shape=out_shape,
        compiler_params=_params(2),
        name="ffn_down",
    )(*args)


def _rope_tables(n_lat, n_ctx):
    t = np.arange(n_lat)
    row = (t // GRID_W).astype(np.float32)
    col = (t % GRID_W).astype(np.float32)
    n_freq = NA_HEAD_DIM // 4
    inv_freq = jnp.asarray(ROPE_THETA, F32) ** (-jnp.arange(n_freq, dtype=F32) / n_freq)
    ang = jnp.concatenate([jnp.asarray(row)[:, None] * inv_freq, jnp.asarray(col)[:, None] * inv_freq], axis=-1)
    cos, sin = jnp.cos(ang), jnp.sin(ang)
    cs = jnp.concatenate([cos, cos], axis=1)
    sn = jnp.concatenate([-sin, sin], axis=1)
    cs = jnp.concatenate([cs, jnp.ones((n_ctx, NA_HEAD_DIM), F32)], axis=0)
    sn = jnp.concatenate([sn, jnp.zeros((n_ctx, NA_HEAD_DIM), F32)], axis=0)
    return cs, sn


def kernel(x, c, ctx, c_ctx, w_mod, b_mod, norm_mix_pre, norm_mix_post, norm_ffn_pre, norm_ffn_post, w_in, na_rpb, ml_conv_w, ml_conv_b, ml_igate_b, ml_fgate_b, ml_norm_g, w_na_proj, w_ml_proj, w_out, w_up, ffn_conv_w, ffn_conv_b, w_down):
    B, S, D = x.shape
    CL = ctx.shape[1]
    T = S + CL
    L = w_mod.shape[0]
    na_w = w_na_proj.shape[1]
    v_w = w_ml_proj.shape[1]
    n_na_heads = na_w // NA_HEAD_DIM
    n_ml_heads = v_w // ML_V_DIM
    qk_w = 2 * n_ml_heads * ML_QK_DIM
    n_gate = 2 * n_ml_heads
    dff = w_down.shape[1]
    assert CL == CHUNK and S % CHUNK == 0 and S % GRID_W == 0 and S // GRID_W >= NA_KEY_ROWS
    assert n_ml_heads <= 4 and B + 1 <= MOD_ROWS
    gate_col = 3 * na_w + qk_w + 2 * v_w
    assert w_in.shape[2] == gate_col + 2 * n_gate + 2 * D

    qk_col = 3 * na_w
    v_col = qk_col + qk_w
    mo_col = v_col + v_w
    gn_col = mo_col + v_w
    gm_col = gn_col + D
    tn = _pick_tile(gm_col + D, (2048, 1024, 512, 256))
    assert all(edge % MXU_COLS == 0 for edge in (na_w, qk_col, v_col, mo_col))
    assert qk_col % qk_w == 0 and v_col % v_w == 0 and mo_col % v_w == 0 and gn_col % D == 0
    tm = _pick_tile(T, (768, 256))
    tn_ff = _pick_tile(dff, (1408, 512, 256))

    crow = jnp.concatenate([c, c_ctx[None], jnp.zeros((MOD_ROWS - B - 1, D), F32)], axis=0)
    mod_all = _modulation(crow, w_mod, b_mod).reshape(L, MOD_ROWS, 6, D)
    mods_all = jnp.stack([mod_all[:, :B], jnp.broadcast_to(mod_all[:, B:B + 1], (L, B, 6, D))], axis=2)

    cs, sn = _rope_tables(S, CL)
    pairs, plan = _na_layout(S // GRID_W)
    pair_all = _na_bias_pairs(na_rpb, pairs)

    w_main_all = jnp.concatenate([w_in[:, :, :gate_col], w_in[:, :, gate_col + 2 * n_gate:]], axis=2).astype(BF16)
    ig = w_in[:, :, gate_col:gate_col + n_gate].reshape(L, D, 2, n_ml_heads)
    fg = w_in[:, :, gate_col + n_gate:gate_col + 2 * n_gate].reshape(L, D, 2, n_ml_heads)
    wg_all = jnp.pad(jnp.concatenate([ig, fg], axis=3), ((0, 0), (0, 0), (0, 0), (0, LANES - n_gate)))
    wg_all = wg_all.reshape(L, D, 2 * LANES).astype(BF16)
    gb = jnp.concatenate([ml_igate_b, ml_fgate_b], axis=2)
    gbc_all = jnp.pad(gb, ((0, 0), (0, 0), (0, LANES - n_gate)))[:, :, None, :]
    qs = jnp.concatenate([jnp.full((1, qk_w // 2), ML_QK_DIM ** -0.5, F32), jnp.ones((1, qk_w // 2), F32)], axis=1)
    w_na_all, w_ml_all, w_out_all = w_na_proj.astype(BF16), w_ml_proj.astype(BF16), w_out.astype(BF16)
    w_up_all, w_down_all = w_up.astype(BF16), w_down.astype(BF16)
    vec = lambda a: a[:, None, :]

    xc = jnp.concatenate([x, ctx], axis=1)
    h = _prenorm(xc, mods_all, vec(norm_mix_pre))
    for l in range(L):
        z, gcol = _in_projection(h, cs, sn, w_main_all, wg_all, ml_conv_w, vec(ml_conv_b), qs, l,
                                 n_lat=S, tm=tm, tn=tn, na_w=na_w, qk_col=qk_col, v_col=v_col, mo_col=mo_col)
        na_o = _neighbourhood_attention(z, pair_all, plan, l, n_lat=S, n_heads=n_na_heads, na_w=na_w)
        ml_h = _mlstm(z, gcol, gbc_all, l, n_heads=n_ml_heads, qk_col=qk_col, v_col=v_col)
        xc, h = _merge(na_o, ml_h, z, xc, mods_all, vec(ml_norm_g), vec(norm_mix_post), vec(norm_ffn_pre),
                       w_na_all, w_ml_all, w_out_all, l,
                       n_heads=n_ml_heads, mo_col=mo_col, gn_col=gn_col, gm_col=gm_col)
        act = _ffn_up(h, w_up_all, ffn_conv_w, vec(ffn_conv_b), l, n_lat=S, tm=tm, tn=tn_ff)
        if l == L - 1:
            return _ffn_down(act, xc, mods_all, vec(norm_ffn_post), w_down_all, None, l, last=True, n_lat=S)
        xc, h = _ffn_down(act, xc, mods_all, vec(norm_ffn_post), w_down_all, vec(norm_mix_pre), l, last=False, n_lat=S)
```

```python
import functools

import numpy as np
import jax
import jax.numpy as jnp
from jax import lax
from jax.experimental import pallas as pl
from jax.experimental.pallas import tpu as pltpu

GRID_W = 64
NA_HEAD_DIM = 128
NA_WIN_H = 8
NA_WIN_W = 16
ROPE_THETA = 10000.0
ML_QK_DIM = 128
ML_V_DIM = 256
EPS = 1e-6
M_INIT = -1e30
NEG_BIG = -1e30

LANES = 128
MXU_COLS = 256
BF16_ROWS = 16
MOD_ROWS = 8
CHUNK = 256
RES_TILE = 384
NA_Q_ROWS = CHUNK // GRID_W
NA_KEY_ROWS = 12
NA_MASKED = 2 * NA_WIN_H - 1
VMEM_LIMIT = 56 * 1024 * 1024

F32 = jnp.float32
BF16 = jnp.bfloat16


def _dot(a, b):
    return jnp.dot(a, b, preferred_element_type=F32)


def _dot_nt(a, b):
    return lax.dot_general(a, b, (((1,), (1,)), ((), ())), preferred_element_type=F32)


def _dot_tn(a, b):
    return lax.dot_general(a, b, (((0,), (0,)), ((), ())), preferred_element_type=F32)


def _sigmoid(x):
    return 0.5 * jnp.tanh(0.5 * x) + 0.5


def _log_sigmoid(x):
    return jnp.minimum(x, 0.0) - jnp.log(1.0 + jnp.exp(-jnp.abs(x)))


def _split3(x):
    hi = x.astype(BF16)
    r1 = x - hi.astype(F32)
    mid = r1.astype(BF16)
    lo = (r1 - mid.astype(F32)).astype(BF16)
    return hi, mid, lo


def _params(n_grid):
    return pltpu.CompilerParams(dimension_semantics=("arbitrary",) * n_grid,
                                vmem_limit_bytes=VMEM_LIMIT)


def _pick_tile(n, candidates):
    for c in candidates:
        if n % c == 0:
            return c
    raise ValueError(f"no tile for {n} among {candidates}")


def _norm_mod(x, g, mod_ref, kind, which):
    var = jnp.mean(x * x, axis=-1, keepdims=True)
    y = x * lax.rsqrt(var + EPS) * g
    return (y * (1.0 + mod_ref[0, 0, kind, which + 1:which + 2, :])
            + mod_ref[0, 0, kind, which:which + 1, :]).astype(BF16)


def _by_modulation(i, n_tiles, tm, n_lat, body):
    split = n_lat - (n_tiles - 1) * tm
    assert 0 <= split < tm and split % BF16_ROWS == 0
    pl.when(i < n_tiles - 1)(functools.partial(body, slice(0, tm), 0))

    @pl.when(i == n_tiles - 1)
    def _():
        if split > 0:
            body(slice(0, split), 0)
        body(slice(split, tm), 1)


def _gated_residual(x, y, post_g, gate):
    var = jnp.mean(y * y, axis=-1, keepdims=True)
    return x + gate * (y * lax.rsqrt(var + EPS) * post_g)


def _mod_kernel(ct_ref, w_ref, b_ref, o_ref, *, n_rows):
    ct = ct_ref[...]
    s = ct * _sigmoid(ct)
    w = w_ref[0]
    rows = [jnp.sum(w * s[:, r:r + 1], axis=0, keepdims=True) for r in range(n_rows)]
    rows.append(jnp.zeros((MOD_ROWS - n_rows, w.shape[1]), F32))
    o_ref[0] = jnp.concatenate(rows, axis=0) + b_ref[0]


def _modulation(ccol, w_mod, b_mod, n_rows):
    L, D, N = w_mod.shape
    tn = _pick_tile(N, (1024, 512, 256, 128))
    return pl.pallas_call(
        functools.partial(_mod_kernel, n_rows=n_rows),
        grid=(L, N // tn),
        in_specs=[pl.BlockSpec((D, LANES), lambda l, j: (0, 0)),
                  pl.BlockSpec((1, D, tn), lambda l, j: (l, 0, j)),
                  pl.BlockSpec((1, 1, tn), lambda l, j: (l, 0, j))],
        out_specs=pl.BlockSpec((1, MOD_ROWS, tn), lambda l, j: (l, 0, j)),
        out_shape=jax.ShapeDtypeStruct((L, MOD_ROWS, N), F32),
        compiler_params=_params(2),
        name="adaln_mod",
    )(ccol, w_mod, b_mod.reshape(L, 1, N))


def _prenorm_kernel(x_ref, mod_ref, g_ref, h_ref):
    h_ref[0] = _norm_mod(x_ref[0], g_ref[0], mod_ref, 0, 0)


def _prenorm(xc, mods_all, g_all):
    B, T, D = xc.shape
    nt = T // CHUNK
    return pl.pallas_call(
        _prenorm_kernel,
        grid=(B, nt),
        in_specs=[pl.BlockSpec((1, CHUNK, D), lambda b, i: (b, i, 0)),
                  pl.BlockSpec((1, 1, 1, 6, D), lambda b, i: (0, b, i // (nt - 1), 0, 0)),
                  pl.BlockSpec((1, 1, D), lambda b, i: (0, 0, 0))],
        out_specs=pl.BlockSpec((1, CHUNK, D), lambda b, i: (b, i, 0)),
        out_shape=jax.ShapeDtypeStruct((B, T, D), BF16),
        compiler_params=_params(2),
        name="prenorm",
    )(xc, mods_all, g_all)


def _fill_h(h_scr, h_ref, hp_ref, hn_ref, tm):
    H = BF16_ROWS
    h_scr[pl.ds(0, H), :] = hp_ref[0]
    h_scr[pl.ds(H, tm), :] = h_ref[0]
    h_scr[pl.ds(H + tm, H), :] = hn_ref[0]


def _seq_edges(i, tm, n_lat, n_tot):
    row = i * tm + lax.broadcasted_iota(jnp.int32, (tm, 1), 0)
    is_start = jnp.logical_or(row == 0, row == n_lat)
    is_end = jnp.logical_or(row == n_lat - 1, row == n_tot - 1)
    return is_start, is_end


def _conv3(u_ref, width, cw, cb, is_start, is_end, tm):
    H = BF16_ROWS
    u_prev = jnp.where(is_start, 0.0, u_ref[pl.ds(H - 1, tm), pl.ds(0, width)])
    u_next = jnp.where(is_end, 0.0, u_ref[pl.ds(H + 1, tm), pl.ds(0, width)])
    return cw[0:1, :] * u_prev + cw[1:2, :] * u_ref[pl.ds(H, tm), pl.ds(0, width)] + cw[2:3, :] * u_next + cb


def _h_specs(tm, T, D):
    hb = tm // BF16_ROWS
    n_hblk = T // BF16_ROWS
    return [pl.BlockSpec((1, tm, D), lambda b, i, j: (b, i, 0)),
            pl.BlockSpec((1, BF16_ROWS, D), lambda b, i, j: (b, jnp.maximum(i * hb - 1, 0), 0)),
            pl.BlockSpec((1, BF16_ROWS, D), lambda b, i, j: (b, jnp.minimum((i + 1) * hb, n_hblk - 1), 0))]


def _inproj_kernel(h_ref, hp_ref, hn_ref, cs_ref, sn_ref, w_ref, wg_ref, cw_ref, cb_ref, qs_ref,
                   z_ref, gcol_ref, grow_ref, h_scr, u_scr, *,
                   n_lat, n_tot, tm, tn, n_col, sub, na_w, qk_col, v_col, mo_col, q_scale):
    i = pl.program_id(1)
    j = pl.program_id(2)

    @pl.when(j == 0)
    def _():
        _fill_h(h_scr, h_ref, hp_ref, hn_ref, tm)
        gates = _dot(h_ref[0], wg_ref[0])
        gcol_ref[0] = gates
        grow_ref[0] = jnp.concatenate([gates[:, :LANES].T[0:8], gates[:, LANES:].T[0:8]], axis=0)

    def col_tile(jj):
        is_start, is_end = _seq_edges(i, tm, n_lat, n_tot)
        for sc in range(tn // sub):
            col = jj * tn + sc * sub
            cols = slice(sc * sub, (sc + 1) * sub)
            if qk_col <= col < v_col:
                u = u_scr.at[(col - qk_col) // sub]
                u[...] = _dot(h_scr[...], w_ref[0, :, cols])
                cc = slice(col - qk_col, col - qk_col + sub)
                y = _conv3(u, sub, cw_ref[0, :, cc], cb_ref[0, :, cc], is_start, is_end, tm)
                z_ref[0, :, cols] = (y * _sigmoid(y) * qs_ref[:, cc]).astype(BF16)
                continue
            acc = _dot(h_ref[0], w_ref[0, :, cols])
            if col < 2 * na_w:
                for c in range(sub // LANES):
                    chunk = acc[:, c * LANES:(c + 1) * LANES]
                    out = chunk * cs_ref[...] + pltpu.roll(chunk, LANES // 2, axis=1) * sn_ref[...]
                    if col < na_w:
                        out = out * q_scale
                    z_ref[0, :, sc * sub + c * LANES:sc * sub + (c + 1) * LANES] = out.astype(BF16)
            elif col < mo_col:
                z_ref[0, :, cols] = acc.astype(BF16)
            else:
                z_ref[0, :, cols] = _sigmoid(acc).astype(BF16)

    for jj in range(n_col):
        pl.when(j == jj)(functools.partial(col_tile, jj))


def _in_projection(h, cs, sn, w_all, wg_all, cw_all, cb_all, qs, layer, *, n_lat, tm, tn, na_w, qk_col, v_col, mo_col):
    B, T, D = h.shape
    NZ = w_all.shape[2]
    sub = min(tn, MXU_COLS)
    qk_w = v_col - qk_col
    kern = functools.partial(_inproj_kernel, n_lat=n_lat, n_tot=T, tm=tm, tn=tn, n_col=NZ // tn, sub=sub, na_w=na_w,
                             qk_col=qk_col, v_col=v_col, mo_col=mo_col, q_scale=NA_HEAD_DIM ** -0.5)
    return pl.pallas_call(
        kern,
        grid=(B, T // tm, NZ // tn),
        in_specs=_h_specs(tm, T, D) + [
                  pl.BlockSpec((tm, LANES), lambda b, i, j: (i, 0)),
                  pl.BlockSpec((tm, LANES), lambda b, i, j: (i, 0)),
                  pl.BlockSpec((1, D, tn), lambda b, i, j: (layer, 0, j)),
                  pl.BlockSpec((1, D, 2 * LANES), lambda b, i, j: (layer, 0, 0)),
                  pl.BlockSpec((1, 3, qk_w), lambda b, i, j: (layer, 0, 0)),
                  pl.BlockSpec((1, 1, qk_w), lambda b, i, j: (layer, 0, 0)),
                  pl.BlockSpec((1, qk_w), lambda b, i, j: (0, 0))],
        out_specs=[pl.BlockSpec((1, tm, tn), lambda b, i, j: (b, i, j)),
                   pl.BlockSpec((1, tm, 2 * LANES), lambda b, i, j: (b, i, 0)),
                   pl.BlockSpec((1, 16, tm), lambda b, i, j: (b, 0, i))],
        out_shape=[jax.ShapeDtypeStruct((B, T, NZ), BF16),
                   jax.ShapeDtypeStruct((B, T, 2 * LANES), F32),
                   jax.ShapeDtypeStruct((B, 16, T), F32)],
        scratch_shapes=[pltpu.VMEM((tm + 2 * BF16_ROWS, D), BF16),
                        pltpu.VMEM((qk_w // sub, tm + 2 * BF16_ROWS, sub), F32)],
        compiler_params=_params(3),
        name="in_proj",
    )(h, h, h, cs, sn, w_all, wg_all, cw_all, cb_all, qs)


def _na_layout(n_rows):
    n_tiles = n_rows // NA_Q_ROWS
    pairs, plan = [], []
    for gt in (0, 1, n_tiles - 1):
        ws = int(np.clip(NA_Q_ROWS * gt - NA_WIN_H // 2, 0, n_rows - NA_KEY_ROWS))
        per_rq = []
        for rq in range(NA_Q_ROWS):
            r = NA_Q_ROWS * gt + rq
            row_start = int(np.clip(r - NA_WIN_H // 2, 0, n_rows - NA_WIN_H))
            per_pair = []
            for p in range(NA_KEY_ROWS // 2):
                ds = []
                for key_row in (ws + 2 * p, ws + 2 * p + 1):
                    ok = row_start <= key_row < row_start + NA_WIN_H
                    ds.append(key_row - r + NA_WIN_H - 1 if ok else NA_MASKED)
                ds = tuple(ds)
                if ds == (NA_MASKED, NA_MASKED):
                    per_pair.append(None)
                    continue
                if ds not in pairs:
                    pairs.append(ds)
                per_pair.append(pairs.index(ds))
            per_rq.append(per_pair)
        plan.append(per_rq)
    return pairs, plan


def _na_bias_pairs(rpb, pairs):
    L, H = rpb.shape[:2]
    kw = NA_WIN_W
    cq = np.arange(GRID_W)[:, None]
    ck = np.arange(GRID_W)[None, :]
    dc = np.clip(ck - cq, 1 - kw, kw - 1) + kw - 1
    col_start = np.clip(cq - kw // 2, 0, GRID_W - kw)
    ok_c = (ck >= col_start) & (ck < col_start + kw)
    onehot = (dc[..., None] == np.arange(2 * kw - 1)).astype(np.float32)
    toe = jnp.einsum("lhdj,qkj->lhdqk", rpb, jnp.asarray(onehot), precision=lax.Precision.HIGHEST)
    toe = jnp.where(jnp.asarray(ok_c), toe, NEG_BIG)
    toe = jnp.concatenate([toe, jnp.full((L, H, 1, GRID_W, GRID_W), NEG_BIG, F32)], axis=2)
    return jnp.stack([jnp.concatenate([toe[:, :, da], toe[:, :, db]], axis=-1) for da, db in pairs], axis=2)


def _na_kernel(q_ref, k_ref, v_ref, pair_ref, o_ref, *, n_lat, n_lat_tiles, win, hpb, plan):
    g = pl.program_id(2)
    n_pairs = NA_KEY_ROWS // 2

    def latent(tile_plan):
        start = pl.multiple_of(jnp.clip((g - 1) * CHUNK, 0, n_lat - win), CHUNK)
        for hh in range(hpb):
            hs = slice(hh * LANES, (hh + 1) * LANES)
            q = q_ref[0, :, hs]
            vc = v_ref[0, pl.ds(n_lat, CHUNK), hs]
            vl = v_ref[0, pl.ds(start, win), hs]
            s_ctx = _dot_nt(q, k_ref[0, pl.ds(n_lat, CHUNK), hs])
            s_loc = _dot_nt(q, k_ref[0, pl.ds(start, win), hs])
            e_rows, ec_rows, inv_rows = [], [], []
            for rq in range(NA_Q_ROWS):
                rs = slice(rq * GRID_W, (rq + 1) * GRID_W)
                blocks = [None if ci is None else
                          s_loc[rs, p * LANES:(p + 1) * LANES] + pair_ref[0, hh, ci]
                          for p, ci in enumerate(tile_plan[rq])]
                live = [blk for blk in blocks if blk is not None]
                sc = s_ctx[rs]
                top = functools.reduce(jnp.maximum, live + [sc[:, :LANES], sc[:, LANES:]])
                m = jnp.max(top, axis=1, keepdims=True)
                e_blocks = [jnp.zeros((GRID_W, LANES), F32) if blk is None else jnp.exp(blk - m) for blk in blocks]
                e_ctx = jnp.exp(sc - m)
                tot = functools.reduce(jnp.add, [e for e, blk in zip(e_blocks, blocks) if blk is not None]
                                       + [e_ctx[:, :LANES], e_ctx[:, LANES:]])
                inv_rows.append(1.0 / jnp.sum(tot, axis=1, keepdims=True))
                e_rows.append(jnp.concatenate(e_blocks, axis=1).astype(BF16))
                ec_rows.append(e_ctx.astype(BF16))
            o = _dot(jnp.concatenate(e_rows, axis=0), vl) + _dot(jnp.concatenate(ec_rows, axis=0), vc)
            o_ref[0, :, hs] = (o * jnp.concatenate(inv_rows, axis=0)).astype(BF16)

    pl.when(g == 0)(functools.partial(latent, plan[0]))
    pl.when(jnp.logical_and(g > 0, g < n_lat_tiles - 1))(functools.partial(latent, plan[1]))
    pl.when(g == n_lat_tiles - 1)(functools.partial(latent, plan[2]))

    @pl.when(g == n_lat_tiles)
    def _():
        for hh in range(hpb):
            hs = slice(hh * LANES, (hh + 1) * LANES)
            vc = v_ref[0, pl.ds(n_lat, CHUNK), hs]
            s_ctx = _dot_nt(q_ref[0, :, hs], k_ref[0, pl.ds(n_lat, CHUNK), hs])
            e_ctx = jnp.exp(s_ctx - jnp.max(s_ctx, axis=1, keepdims=True))
            denom = jnp.sum(e_ctx, axis=1, keepdims=True)
            o_ref[0, :, hs] = (_dot(e_ctx.astype(BF16), vc) / denom).astype(BF16)


def _neighbourhood_attention(z, pair_all, plan, layer, *, n_lat, n_heads, na_w):
    B, T, _ = z.shape
    n_lat_tiles = n_lat // CHUNK
    win = NA_KEY_ROWS * GRID_W
    hpb = _pick_tile(n_heads, (4, 2, 1))
    n_hg = n_heads // hpb
    n_pair_kinds = pair_all.shape[2]
    kern = functools.partial(_na_kernel, n_lat=n_lat, n_lat_tiles=n_lat_tiles, win=win, hpb=hpb, plan=plan)
    return pl.pallas_call(
        kern,
        grid=(B, n_hg, n_lat_tiles + 1),
        in_specs=[pl.BlockSpec((1, CHUNK, hpb * LANES), lambda b, h, g: (b, g, h)),
                  pl.BlockSpec((1, T, hpb * LANES), lambda b, h, g: (b, 0, n_hg + h)),
                  pl.BlockSpec((1, T, hpb * LANES), lambda b, h, g: (b, 0, 2 * n_hg + h)),
                  pl.BlockSpec((1, hpb, n_pair_kinds, GRID_W, LANES), lambda b, h, g: (layer, h, 0, 0, 0))],
        out_specs=pl.BlockSpec((1, CHUNK, hpb * LANES), lambda b, h, g: (b, g, h)),
        out_shape=jax.ShapeDtypeStruct((B, T, na_w), BF16),
        compiler_params=_params(3),
        name="na_attn",
    )(z, z, z, pair_all)


def _ml_chunk(n, d, n_chunks):
    return jnp.where(n == 0, n_chunks - 1, jnp.where(d == 0, n - 1, n_chunks - 1 - n))


def _mlstm_kernel(qk_ref, v_ref, gcol_ref, grow_ref, gbc_ref, gbr_ref, o_ref, c_scr, m_scr, *, n_heads):
    d = pl.program_id(1)
    n = pl.program_id(2)
    L = CHUNK

    @pl.when(n == 0)
    def _():
        c_scr[...] = jnp.zeros_like(c_scr)
        m_scr[...] = jnp.full_like(m_scr, M_INIT)

    ga = gcol_ref[0] + gbc_ref[0, 0]
    lf_c = _log_sigmoid(ga)
    gr = grow_ref[0] + gbr_ref[0, 0][:, 0:1]
    lf_r = _log_sigmoid(gr)

    sgn = 1 - 2 * d
    ti = lax.broadcasted_iota(jnp.int32, (L, L), 0)
    si = lax.broadcasted_iota(jnp.int32, (L, L), 1)
    mask = ((si - ti) * sgn) <= 0
    maskb = mask.astype(F32).astype(BF16)
    hi, mid, lo = _split3(lf_c)
    b_c = _dot(maskb, hi) + _dot(maskb, mid) + _dot(maskb, lo)
    hi, mid, lo = _split3(jnp.concatenate([lf_r, jnp.zeros_like(lf_r)], axis=0))
    b_r = _dot_nt(hi, maskb) + _dot_nt(mid, maskb) + _dot_nt(lo, maskb)

    ones = jnp.ones((L, LANES), BF16)
    for hh in range(n_heads):
        q = qk_ref[0, :, hh * ML_QK_DIM:(hh + 1) * ML_QK_DIM]
        k = qk_ref[0, :, (n_heads + hh) * ML_QK_DIM:(n_heads + hh + 1) * ML_QK_DIM]
        vext = jnp.concatenate([v_ref[0, :, hh * ML_V_DIM:(hh + 1) * ML_V_DIM], ones], axis=1)
        li_c = ga[:, hh:hh + 1]
        b_col = b_c[:, n_heads + hh:n_heads + hh + 1]
        li_r = gr[hh:hh + 1, :]
        b_row = b_r[n_heads + hh:n_heads + hh + 1, :]
        g_tot = jnp.sum(lf_r[n_heads + hh:n_heads + hh + 1, :], axis=1, keepdims=True)
        m_prev = m_scr[hh, 0:1, 0:1]
        c_prev = c_scr[hh]

        dm = jnp.where(mask, b_col + (li_r - b_row), NEG_BIG)
        inter = b_col + m_prev
        m_t = jnp.maximum(inter, jnp.max(dm, axis=1, keepdims=True))
        w_inter = jnp.exp(inter - m_t)
        p = jnp.exp(dm - m_t)
        s = (_dot_nt(q, k) * p).astype(BF16)
        r = _dot(s, vext) + w_inter * _dot(q, c_prev.astype(BF16))
        den = jnp.maximum(jnp.abs(r[:, ML_V_DIM:ML_V_DIM + 1]), jnp.exp(-m_t))
        o_ref[0, 0, :, hh * ML_V_DIM:(hh + 1) * ML_V_DIM] = (r[:, :ML_V_DIM] / den).astype(BF16)

        a_col = g_tot + li_c - b_col
        ma = jnp.max(a_col, axis=0, keepdims=True)
        wk = (jnp.exp(a_col - ma) * k.astype(F32)).astype(BF16)
        c_loc = _dot_tn(wk, vext)
        m_new = jnp.maximum(g_tot + m_prev, ma)
        dec = jnp.exp(g_tot + m_prev - m_new)
        inc = jnp.exp(ma - m_new)
        c_scr[hh] = dec * c_prev + inc * c_loc
        m_scr[hh] = jnp.broadcast_to(m_new, m_scr.shape[1:])


def _mlstm(z, gcol, grow, gbc_all, gbr_all, layer, *, n_heads, qk_col, v_col):
    B, T, _ = z.shape
    n_chunks = T // CHUNK
    qk_w = 2 * n_heads * ML_QK_DIM
    v_w = n_heads * ML_V_DIM
    qk_blk = qk_col // qk_w
    v_blk = v_col // v_w

    def chunk(n, d):
        return _ml_chunk(n, d, n_chunks)

    kern = functools.partial(_mlstm_kernel, n_heads=n_heads)
    return pl.pallas_call(
        kern,
        grid=(B, 2, n_chunks),
        in_specs=[pl.BlockSpec((1, CHUNK, qk_w), lambda b, d, n: (b, chunk(n, d), qk_blk)),
                  pl.BlockSpec((1, CHUNK, v_w), lambda b, d, n: (b, chunk(n, d), v_blk)),
                  pl.BlockSpec((1, CHUNK, LANES), lambda b, d, n: (b, chunk(n, d), d)),
                  pl.BlockSpec((1, 8, CHUNK), lambda b, d, n: (b, d, chunk(n, d))),
                  pl.BlockSpec((1, 1, 1, LANES), lambda b, d, n: (layer, d, 0, 0)),
                  pl.BlockSpec((1, 1, 8, LANES), lambda b, d, n: (layer, d, 0, 0))],
        out_specs=pl.BlockSpec((1, 1, CHUNK, v_w), lambda b, d, n: (d, b, chunk(n, d), 0)),
        out_shape=jax.ShapeDtypeStruct((2, B, T, v_w), BF16),
        scratch_shapes=[pltpu.VMEM((n_heads, ML_QK_DIM, ML_V_DIM + LANES), F32),
                        pltpu.VMEM((n_heads, 8, LANES), F32)],
        compiler_params=_params(3),
        name="mlstm",
    )(z, z, gcol, grow, gbc_all, gbr_all)


def _merge_kernel(na_ref, hf_ref, hb_ref, mo_ref, gn_ref, gm_ref, x_ref, mod_ref, ng_ref, pg_ref, fg_ref,
                  wna_ref, wml_ref, wo_ref, o_ref, h_ref, *, n_heads, n_tiles, tm, n_lat):
    hs = hf_ref[0, 0].astype(F32) + hb_ref[0, 0].astype(F32)
    parts = []
    for hh in range(n_heads):
        hv = hs[:, hh * ML_V_DIM:(hh + 1) * ML_V_DIM]
        var = jnp.mean(hv * hv, axis=-1, keepdims=True)
        parts.append(hv * lax.rsqrt(var + EPS))
    ml = jnp.concatenate(parts, axis=1) * ng_ref[0] * mo_ref[0].astype(F32)
    y = (gn_ref[0].astype(F32) * _dot(na_ref[0], wna_ref[0])
         + gm_ref[0].astype(F32) * _dot(ml.astype(BF16), wml_ref[0]))
    yo = _dot(y.astype(BF16), wo_ref[0])

    def epilogue(rows, kind):
        x_new = _gated_residual(x_ref[0, rows, :], yo[rows], pg_ref[0], mod_ref[0, 0, kind, 2:3, :])
        o_ref[0, rows, :] = x_new
        h_ref[0, rows, :] = _norm_mod(x_new, fg_ref[0], mod_ref, kind, 3)

    _by_modulation(pl.program_id(1), n_tiles, tm, n_lat, epilogue)


def _merge(na_o, ml_h, z, xc, mods_all, ng_all, pg_all, fg_all, wna_all, wml_all, wo_all, layer, *,
           n_heads, n_lat, mo_col, gn_col, gm_col):
    B, T, D = xc.shape
    na_w = na_o.shape[2]
    v_w = ml_h.shape[3]
    tm = _pick_tile(T, (RES_TILE, CHUNK))
    nt = T // tm
    kern = functools.partial(_merge_kernel, n_heads=n_heads, n_tiles=nt, tm=tm, n_lat=n_lat)
    const = dict(pipeline_mode=pl.Buffered(1))
    row = lambda b, i: (b, i, 0)
    vec = lambda b, i: (layer, 0, 0)
    return pl.pallas_call(
        kern,
        grid=(B, nt),
        in_specs=[pl.BlockSpec((1, tm, na_w), row),
                  pl.BlockSpec((1, 1, tm, v_w), lambda b, i: (0, b, i, 0)),
                  pl.BlockSpec((1, 1, tm, v_w), lambda b, i: (1, b, i, 0)),
                  pl.BlockSpec((1, tm, v_w), lambda b, i: (b, i, mo_col // v_w)),
                  pl.BlockSpec((1, tm, D), lambda b, i: (b, i, gn_col // D)),
                  pl.BlockSpec((1, tm, D), lambda b, i: (b, i, gm_col // D)),
                  pl.BlockSpec((1, tm, D), row),
                  pl.BlockSpec((1, 1, 2, 6, D), lambda b, i: (layer, b, 0, 0, 0)),
                  pl.BlockSpec((1, 1, v_w), vec),
                  pl.BlockSpec((1, 1, D), vec),
                  pl.BlockSpec((1, 1, D), vec),
                  pl.BlockSpec((1, na_w, D), vec, **const),
                  pl.BlockSpec((1, v_w, D), vec, **const),
                  pl.BlockSpec((1, D, D), vec, **const)],
        out_specs=[pl.BlockSpec((1, tm, D), row), pl.BlockSpec((1, tm, D), row)],
        out_shape=[jax.ShapeDtypeStruct((B, T, D), F32), jax.ShapeDtypeStruct((B, T, D), BF16)],
        compiler_params=_params(2),
        name="merge_out_proj",
    )(na_o, ml_h, ml_h, z, z, z, xc, mods_all, ng_all, pg_all, fg_all, wna_all, wml_all, wo_all)


def _ffn_sub_blocks(tn):
    return [(c0, min(MXU_COLS, tn - c0)) for c0 in range(0, tn, MXU_COLS)]


def _interleave_gate_value(w_up, tn):
    dff = w_up.shape[2] // 2
    pieces = []
    for t0 in range(0, dff, tn):
        for c0, width in _ffn_sub_blocks(tn):
            pieces += [w_up[:, :, t0 + c0:t0 + c0 + width], w_up[:, :, dff + t0 + c0:dff + t0 + c0 + width]]
    return jnp.concatenate(pieces, axis=2).astype(BF16)


def _ffn_up_kernel(h_ref, hp_ref, hn_ref, w_ref, cw_ref, cb_ref, o_ref, h_scr, u_scr, *, n_lat, n_tot, tm, tn):
    i = pl.program_id(1)
    j = pl.program_id(2)
    H = BF16_ROWS

    @pl.when(j == 0)
    def _():
        _fill_h(h_scr, h_ref, hp_ref, hn_ref, tm)

    is_start, is_end = _seq_edges(i, tm, n_lat, n_tot)
    for sc, (c0, width) in enumerate(_ffn_sub_blocks(tn)):
        cols = slice(c0, c0 + width)
        both = _dot(h_scr[...], w_ref[0, :, 2 * c0:2 * (c0 + width)])
        u = u_scr.at[sc]
        u[:, pl.ds(0, width)] = both[:, :width]
        val = both[H:H + tm, width:]
        y = _conv3(u, width, cw_ref[0, :, cols], cb_ref[0, :, cols], is_start, is_end, tm)
        o_ref[0, :, cols] = (y * _sigmoid(y) * val).astype(BF16)


def _ffn_up(h, w_all, cw_all, cb_all, layer, *, n_lat, tm, tn):
    B, T, D = h.shape
    dff = w_all.shape[2] // 2
    nj = dff // tn
    sub = MXU_COLS
    kern = functools.partial(_ffn_up_kernel, n_lat=n_lat, n_tot=T, tm=tm, tn=tn)
    return pl.pallas_call(
        kern,
        grid=(B, T // tm, nj),
        in_specs=_h_specs(tm, T, D) + [
                  pl.BlockSpec((1, D, 2 * tn), lambda b, i, j: (layer, 0, j)),
                  pl.BlockSpec((1, 3, tn), lambda b, i, j: (layer, 0, j)),
                  pl.BlockSpec((1, 1, tn), lambda b, i, j: (layer, 0, j))],
        out_specs=pl.BlockSpec((1, tm, tn), lambda b, i, j: (b, i, j)),
        out_shape=jax.ShapeDtypeStruct((B, T, dff), BF16),
        scratch_shapes=[pltpu.VMEM((tm + 2 * BF16_ROWS, D), BF16),
                        pltpu.VMEM((pl.cdiv(tn, sub), tm + 2 * BF16_ROWS, sub), F32)],
        compiler_params=_params(3),
        name="ffn_up",
    )(h, h, h, w_all, cw_all, cb_all)


def _ffn_down_kernel(a_ref, x_ref, mod_ref, pg_ref, w_ref, *rest, emit_h, n_tiles, tm, n_lat):
    y = _dot(a_ref[0], w_ref[0])
    if emit_h:
        modn_ref, gn_ref, o_ref, h_ref = rest
    else:
        (o_ref,) = rest

    def epilogue(rows, kind):
        x_new = _gated_residual(x_ref[0, rows, :], y[rows], pg_ref[0], mod_ref[0, 0, kind, 5:6, :])
        o_ref[0, rows, :] = x_new
        if emit_h:
            h_ref[0, rows, :] = _norm_mod(x_new, gn_ref[0], modn_ref, kind, 0)

    _by_modulation(pl.program_id(1), n_tiles, tm, n_lat, epilogue)


def _ffn_down(act, xc, mods_all, pg_all, w_all, gpre_all, layer, *, last, n_lat):
    B, T, D = xc.shape
    dff = act.shape[2]
    tm = CHUNK if last else _pick_tile(T, (RES_TILE, CHUNK))
    nt = T // tm
    row = lambda b, i: (b, i, 0)
    mod = lambda lyr: pl.BlockSpec((1, 1, 2, 6, D), lambda b, i: (lyr, b, 0, 0, 0))
    in_specs = [pl.BlockSpec((1, tm, dff), row),
                pl.BlockSpec((1, tm, D), row),
                mod(layer),
                pl.BlockSpec((1, 1, D), lambda b, i: (layer, 0, 0)),
                pl.BlockSpec((1, dff, D), lambda b, i: (layer, 0, 0), pipeline_mode=pl.Buffered(1))]
    args = [act, xc, mods_all, pg_all, w_all]
    if last:
        assert n_lat % tm == 0
        grid = (B, n_lat // tm)
        out_specs = pl.BlockSpec((1, tm, D), row)
        out_shape = jax.ShapeDtypeStruct((B, n_lat, D), F32)
    else:
        grid = (B, nt)
        in_specs += [mod(layer + 1), pl.BlockSpec((1, 1, D), lambda b, i: (layer + 1, 0, 0))]
        args += [mods_all, gpre_all]
        out_specs = [pl.BlockSpec((1, tm, D), row), pl.BlockSpec((1, tm, D), row)]
        out_shape = [jax.ShapeDtypeStruct((B, T, D), F32), jax.ShapeDtypeStruct((B, T, D), BF16)]
    return pl.pallas_call(
        functools.partial(_ffn_down_kernel, emit_h=not last, n_tiles=nt, tm=tm, n_lat=n_lat),
        grid=grid, in_specs=in_specs, out_specs=out_specs, out_shape=out_shape,
        compiler_params=_params(2),
        name="ffn_down",
    )(*args)


def _rope_tables(n_lat, n_ctx):
    t = np.arange(n_lat)
    row = (t // GRID_W).astype(np.float32)
    col = (t % GRID_W).astype(np.float32)
    n_freq = NA_HEAD_DIM // 4
    inv_freq = jnp.asarray(ROPE_THETA, F32) ** (-jnp.arange(n_freq, dtype=F32) / n_freq)
    ang = jnp.concatenate([jnp.asarray(row)[:, None] * inv_freq, jnp.asarray(col)[:, None] * inv_freq], axis=-1)
    cos, sin = jnp.cos(ang), jnp.sin(ang)
    cs = jnp.concatenate([cos, cos], axis=1)
    sn = jnp.concatenate([-sin, sin], axis=1)
    cs = jnp.concatenate([cs, jnp.ones((n_ctx, NA_HEAD_DIM), F32)], axis=0)
    sn = jnp.concatenate([sn, jnp.zeros((n_ctx, NA_HEAD_DIM), F32)], axis=0)
    return cs, sn


def kernel(x, c, ctx, c_ctx, w_mod, b_mod, norm_mix_pre, norm_mix_post, norm_ffn_pre, norm_ffn_post, w_in, na_rpb, ml_conv_w, ml_conv_b, ml_igate_b, ml_fgate_b, ml_norm_g, w_na_proj, w_ml_proj, w_out, w_up, ffn_conv_w, ffn_conv_b, w_down):
    B, S, D = x.shape
    CL = ctx.shape[1]
    T = S + CL
    L = w_mod.shape[0]
    na_w = w_na_proj.shape[1]
    v_w = w_ml_proj.shape[1]
    n_na_heads = na_w // NA_HEAD_DIM
    n_ml_heads = v_w // ML_V_DIM
    qk_w = 2 * n_ml_heads * ML_QK_DIM
    n_gate = 2 * n_ml_heads
    dff = w_down.shape[1]
    assert CL == CHUNK and S % CHUNK == 0 and S % GRID_W == 0 and S // GRID_W >= NA_KEY_ROWS
    assert n_ml_heads <= 4 and B + 1 <= MOD_ROWS
    gate_col = 3 * na_w + qk_w + 2 * v_w
    assert w_in.shape[2] == gate_col + 2 * n_gate + 2 * D

    qk_col = 3 * na_w
    v_col = qk_col + qk_w
    mo_col = v_col + v_w
    gn_col = mo_col + v_w
    gm_col = gn_col + D
    tn = _pick_tile(gm_col + D, (2048, 1024, 512, 256))
    assert all(edge % MXU_COLS == 0 for edge in (na_w, qk_col, v_col, mo_col))
    assert qk_col % qk_w == 0 and v_col % v_w == 0 and mo_col % v_w == 0 and gn_col % D == 0
    tm = _pick_tile(T, (768, 256))
    tn_ff = _pick_tile(dff, (1408, 512, 256))

    ccol = jnp.pad(jnp.concatenate([c, c_ctx[None]], axis=0).T, ((0, 0), (0, LANES - B - 1)))
    mod_all = _modulation(ccol, w_mod, b_mod, B + 1).reshape(L, MOD_ROWS, 6, D)
    mods_all = jnp.stack([mod_all[:, :B], jnp.broadcast_to(mod_all[:, B:B + 1], (L, B, 6, D))], axis=2)

    cs, sn = _rope_tables(S, CL)
    pairs, plan = _na_layout(S // GRID_W)
    pair_all = _na_bias_pairs(na_rpb, pairs)

    w_main_all = jnp.concatenate([w_in[:, :, :gate_col], w_in[:, :, gate_col + 2 * n_gate:]], axis=2).astype(BF16)
    ig = w_in[:, :, gate_col:gate_col + n_gate].reshape(L, D, 2, n_ml_heads)
    fg = w_in[:, :, gate_col + n_gate:gate_col + 2 * n_gate].reshape(L, D, 2, n_ml_heads)
    wg_all = jnp.pad(jnp.concatenate([ig, fg], axis=3), ((0, 0), (0, 0), (0, 0), (0, LANES - n_gate)))
    wg_all = wg_all.reshape(L, D, 2 * LANES).astype(BF16)
    gb = jnp.concatenate([ml_igate_b, ml_fgate_b], axis=2)
    gbc_all = jnp.pad(gb, ((0, 0), (0, 0), (0, LANES - n_gate)))[:, :, None, :]
    gbr_all = jnp.broadcast_to(jnp.pad(gb, ((0, 0), (0, 0), (0, 8 - n_gate)))[..., None], (L, 2, 8, LANES))
    qs = jnp.concatenate([jnp.full((1, qk_w // 2), ML_QK_DIM ** -0.5, F32), jnp.ones((1, qk_w // 2), F32)], axis=1)
    w_na_all, w_ml_all, w_out_all = w_na_proj.astype(BF16), w_ml_proj.astype(BF16), w_out.astype(BF16)
    w_up_all, w_down_all = _interleave_gate_value(w_up, tn_ff), w_down.astype(BF16)
    vec = lambda a: a[:, None, :]

    xc = jnp.concatenate([x, ctx], axis=1)
    h = _prenorm(xc, mods_all, vec(norm_mix_pre))
    for l in range(L):
        z, gcol, grow = _in_projection(h, cs, sn, w_main_all, wg_all, ml_conv_w, vec(ml_conv_b), qs, l,
                                       n_lat=S, tm=tm, tn=tn, na_w=na_w, qk_col=qk_col, v_col=v_col, mo_col=mo_col)
        na_o = _neighbourhood_attention(z, pair_all, plan, l, n_lat=S, n_heads=n_na_heads, na_w=na_w)
        ml_h = _mlstm(z, gcol, grow, gbc_all, gbr_all, l, n_heads=n_ml_heads, qk_col=qk_col, v_col=v_col)
        xc, h = _merge(na_o, ml_h, z, xc, mods_all, vec(ml_norm_g), vec(norm_mix_post), vec(norm_ffn_pre),
                       w_na_all, w_ml_all, w_out_all, l,
                       n_heads=n_ml_heads, n_lat=S, mo_col=mo_col, gn_col=gn_col, gm_col=gm_col)
        act = _ffn_up(h, w_up_all, ffn_conv_w, vec(ffn_conv_b), l, n_lat=S, tm=tm, tn=tn_ff)
        if l == L - 1:
            return _ffn_down(act, xc, mods_all, vec(norm_ffn_post), w_down_all, None, l, last=True, n_lat=S)
        xc, h = _ffn_down(act, xc, mods_all, vec(norm_ffn_post), w_down_all, vec(norm_mix_pre), l, last=False, n_lat=S)
```

```python
import functools

import numpy as np
import jax
import jax.numpy as jnp
from jax import lax
from jax.experimental import pallas as pl
from jax.experimental.pallas import tpu as pltpu

GRID_W = 64
NA_HEAD_DIM = 128
NA_WIN_H = 8
NA_WIN_W = 16
ROPE_THETA = 10000.0
ML_QK_DIM = 128
ML_V_DIM = 256
EPS = 1e-6
M_INIT = -1e30
NEG_BIG = -1e30

LANES = 128
MXU_COLS = 256
BF16_ROWS = 16
MOD_ROWS = 8
CHUNK = 256
RES_TILE = 384
NA_Q_ROWS = CHUNK // GRID_W
NA_KEY_ROWS = 12
NA_MASKED = 2 * NA_WIN_H - 1
VMEM_LIMIT = 56 * 1024 * 1024

F32 = jnp.float32
BF16 = jnp.bfloat16


def _dot(a, b):
    return jnp.dot(a, b, preferred_element_type=F32)


def _dot_nt(a, b):
    return lax.dot_general(a, b, (((1,), (1,)), ((), ())), preferred_element_type=F32)


def _dot_tn(a, b):
    return lax.dot_general(a, b, (((0,), (0,)), ((), ())), preferred_element_type=F32)


def _sigmoid(x):
    return 0.5 * jnp.tanh(0.5 * x) + 0.5


def _log_sigmoid(x):
    return jnp.minimum(x, 0.0) - jnp.log(1.0 + jnp.exp(-jnp.abs(x)))


def _split3(x):
    hi = x.astype(BF16)
    r1 = x - hi.astype(F32)
    mid = r1.astype(BF16)
    lo = (r1 - mid.astype(F32)).astype(BF16)
    return hi, mid, lo


def _params(n_grid):
    return pltpu.CompilerParams(dimension_semantics=("arbitrary",) * n_grid,
                                vmem_limit_bytes=VMEM_LIMIT)


def _pick_tile(n, candidates):
    for c in candidates:
        if n % c == 0:
            return c
    raise ValueError(f"no tile for {n} among {candidates}")


def _norm_mod(x, g, mod_ref, kind, which):
    var = jnp.mean(x * x, axis=-1, keepdims=True)
    y = x * lax.rsqrt(var + EPS) * g
    return (y * (1.0 + mod_ref[0, 0, kind, which + 1:which + 2, :])
            + mod_ref[0, 0, kind, which:which + 1, :]).astype(BF16)


def _by_modulation(i, n_tiles, tm, n_lat, body):
    split = n_lat - (n_tiles - 1) * tm
    assert 0 <= split < tm and split % BF16_ROWS == 0
    pl.when(i < n_tiles - 1)(functools.partial(body, slice(0, tm), 0))

    @pl.when(i == n_tiles - 1)
    def _():
        if split > 0:
            body(slice(0, split), 0)
        body(slice(split, tm), 1)


def _gated_residual(x, y, post_g, gate):
    var = jnp.mean(y * y, axis=-1, keepdims=True)
    return x + gate * (y * lax.rsqrt(var + EPS) * post_g)


def _mod_kernel(ct_ref, w_ref, b_ref, o_ref, *, n_rows):
    ct = ct_ref[...]
    s = ct * _sigmoid(ct)
    w = w_ref[0]
    rows = [jnp.sum(w * s[:, r:r + 1], axis=0, keepdims=True) for r in range(n_rows)]
    rows.append(jnp.zeros((MOD_ROWS - n_rows, w.shape[1]), F32))
    o_ref[0] = jnp.concatenate(rows, axis=0) + b_ref[0]


def _modulation(ccol, w_mod, b_mod, n_rows):
    L, D, N = w_mod.shape
    tn = _pick_tile(N, (1024, 512, 256, 128))
    return pl.pallas_call(
        functools.partial(_mod_kernel, n_rows=n_rows),
        grid=(L, N // tn),
        in_specs=[pl.BlockSpec((D, LANES), lambda l, j: (0, 0)),
                  pl.BlockSpec((1, D, tn), lambda l, j: (l, 0, j)),
                  pl.BlockSpec((1, 1, tn), lambda l, j: (l, 0, j))],
        out_specs=pl.BlockSpec((1, MOD_ROWS, tn), lambda l, j: (l, 0, j)),
        out_shape=jax.ShapeDtypeStruct((L, MOD_ROWS, N), F32),
        compiler_params=_params(2),
        name="adaln_mod",
    )(ccol, w_mod, b_mod.reshape(L, 1, N))


def _prenorm_kernel(x_ref, ctx_ref, mod_ref, g_ref, xc_ref, h_ref, *, n_lat_tiles):
    def emit(src_ref):
        xc_ref[0] = src_ref[0]
        h_ref[0] = _norm_mod(src_ref[0], g_ref[0], mod_ref, 0, 0)

    i = pl.program_id(1)
    pl.when(i < n_lat_tiles)(functools.partial(emit, x_ref))
    pl.when(i == n_lat_tiles)(functools.partial(emit, ctx_ref))


def _prenorm(x, ctx, mods_all, g_all):
    B, S, D = x.shape
    T = S + ctx.shape[1]
    nt = T // CHUNK
    row = lambda b, i: (b, i, 0)
    return pl.pallas_call(
        functools.partial(_prenorm_kernel, n_lat_tiles=nt - 1),
        grid=(B, nt),
        in_specs=[pl.BlockSpec((1, CHUNK, D), lambda b, i: (b, jnp.minimum(i, nt - 2), 0)),
                  pl.BlockSpec((1, CHUNK, D), lambda b, i: (b, 0, 0)),
                  pl.BlockSpec((1, 1, 1, 6, D), lambda b, i: (0, b, i // (nt - 1), 0, 0)),
                  pl.BlockSpec((1, 1, D), lambda b, i: (0, 0, 0))],
        out_specs=[pl.BlockSpec((1, CHUNK, D), row), pl.BlockSpec((1, CHUNK, D), row)],
        out_shape=[jax.ShapeDtypeStruct((B, T, D), F32), jax.ShapeDtypeStruct((B, T, D), BF16)],
        compiler_params=_params(2),
        name="prenorm",
    )(x, ctx, mods_all, g_all)


def _fill_h(h_scr, h_ref, hp_ref, hn_ref, tm):
    H = BF16_ROWS
    h_scr[pl.ds(0, H), :] = hp_ref[0]
    h_scr[pl.ds(H, tm), :] = h_ref[0]
    h_scr[pl.ds(H + tm, H), :] = hn_ref[0]


def _seq_edges(i, tm, n_lat, n_tot):
    row = i * tm + lax.broadcasted_iota(jnp.int32, (tm, 1), 0)
    is_start = jnp.logical_or(row == 0, row == n_lat)
    is_end = jnp.logical_or(row == n_lat - 1, row == n_tot - 1)
    return is_start, is_end


def _conv3(u_ref, width, cw, cb, is_start, is_end, tm):
    H = BF16_ROWS
    u_prev = jnp.where(is_start, 0.0, u_ref[pl.ds(H - 1, tm), pl.ds(0, width)])
    u_next = jnp.where(is_end, 0.0, u_ref[pl.ds(H + 1, tm), pl.ds(0, width)])
    return cw[0:1, :] * u_prev + cw[1:2, :] * u_ref[pl.ds(H, tm), pl.ds(0, width)] + cw[2:3, :] * u_next + cb


def _h_specs(tm, T, D):
    hb = tm // BF16_ROWS
    n_hblk = T // BF16_ROWS
    return [pl.BlockSpec((1, tm, D), lambda b, i, j: (b, i, 0)),
            pl.BlockSpec((1, BF16_ROWS, D), lambda b, i, j: (b, jnp.maximum(i * hb - 1, 0), 0)),
            pl.BlockSpec((1, BF16_ROWS, D), lambda b, i, j: (b, jnp.minimum((i + 1) * hb, n_hblk - 1), 0))]


def _inproj_kernel(h_ref, hp_ref, hn_ref, cs_ref, sn_ref, w_ref, wb_ref, wg_ref, cw_ref, cb_ref, qs_ref,
                   z_ref, zb_ref, gcol_ref, grow_ref, h_scr, u_scr, *,
                   n_lat, n_tot, tm, tn, n_col, tn_b, n_col_b, sub, na_w, qk_col, v_col, mo_col, q_scale):
    i = pl.program_id(1)
    j = pl.program_id(2)

    @pl.when(j == 0)
    def _():
        _fill_h(h_scr, h_ref, hp_ref, hn_ref, tm)
        gates = _dot(h_ref[0], wg_ref[0])
        gcol_ref[0] = gates
        grow_ref[0] = jnp.concatenate([gates[:, :LANES].T[0:8], gates[:, LANES:].T[0:8]], axis=0)

    def col_tile(jj):
        is_start, is_end = _seq_edges(i, tm, n_lat, n_tot)
        for sc in range(tn // sub):
            col = jj * tn + sc * sub
            cols = slice(sc * sub, (sc + 1) * sub)
            if qk_col <= col < v_col:
                u = u_scr.at[(col - qk_col) // sub]
                u[...] = _dot(h_scr[...], w_ref[0, :, cols])
                cc = slice(col - qk_col, col - qk_col + sub)
                y = _conv3(u, sub, cw_ref[0, :, cc], cb_ref[0, :, cc], is_start, is_end, tm)
                z_ref[0, :, cols] = (y * _sigmoid(y) * qs_ref[:, cc]).astype(BF16)
                continue
            acc = _dot(h_ref[0], w_ref[0, :, cols])
            if col < 2 * na_w:
                for c in range(sub // LANES):
                    chunk = acc[:, c * LANES:(c + 1) * LANES]
                    out = chunk * cs_ref[...] + pltpu.roll(chunk, LANES // 2, axis=1) * sn_ref[...]
                    if col < na_w:
                        out = out * q_scale
                    z_ref[0, :, sc * sub + c * LANES:sc * sub + (c + 1) * LANES] = out.astype(BF16)
            elif col < mo_col:
                z_ref[0, :, cols] = acc.astype(BF16)
            else:
                z_ref[0, :, cols] = _sigmoid(acc).astype(BF16)

    def gate_tile():
        for sc in range(tn_b // sub):
            cols = slice(sc * sub, (sc + 1) * sub)
            zb_ref[0, :, cols] = _sigmoid(_dot(h_ref[0], wb_ref[0, :, cols])).astype(BF16)

    for jj in range(n_col):
        pl.when(j == jj)(functools.partial(col_tile, jj))
    pl.when(j >= n_col)(gate_tile)


def _in_projection(h, cs, sn, w_all, wb_all, wg_all, cw_all, cb_all, qs, layer, *,
                   n_lat, tm, tn, tn_b, na_w, qk_col, v_col, mo_col):
    B, T, D = h.shape
    NZ, NB = w_all.shape[2], wb_all.shape[2]
    n_col, n_col_b = NZ // tn, NB // tn_b
    sub = min(tn, tn_b, MXU_COLS)
    qk_w = v_col - qk_col
    kern = functools.partial(_inproj_kernel, n_lat=n_lat, n_tot=T, tm=tm, tn=tn, n_col=n_col, tn_b=tn_b,
                             n_col_b=n_col_b, sub=sub, na_w=na_w, qk_col=qk_col, v_col=v_col, mo_col=mo_col,
                             q_scale=NA_HEAD_DIM ** -0.5)
    col_a = lambda j: jnp.minimum(j, n_col - 1)
    col_b = lambda j: jnp.maximum(j - n_col, 0)
    return pl.pallas_call(
        kern,
        grid=(B, T // tm, n_col + n_col_b),
        in_specs=_h_specs(tm, T, D) + [
                  pl.BlockSpec((tm, LANES), lambda b, i, j: (i, 0)),
                  pl.BlockSpec((tm, LANES), lambda b, i, j: (i, 0)),
                  pl.BlockSpec((1, D, tn), lambda b, i, j: (layer, 0, col_a(j))),
                  pl.BlockSpec((1, D, tn_b), lambda b, i, j: (layer, 0, col_b(j))),
                  pl.BlockSpec((1, D, 2 * LANES), lambda b, i, j: (layer, 0, 0)),
                  pl.BlockSpec((1, 3, qk_w), lambda b, i, j: (layer, 0, 0)),
                  pl.BlockSpec((1, 1, qk_w), lambda b, i, j: (layer, 0, 0)),
                  pl.BlockSpec((1, qk_w), lambda b, i, j: (0, 0))],
        out_specs=[pl.BlockSpec((1, tm, tn), lambda b, i, j: (b, i, col_a(j))),
                   pl.BlockSpec((1, tm, tn_b), lambda b, i, j: (b, i, col_b(j))),
                   pl.BlockSpec((1, tm, 2 * LANES), lambda b, i, j: (b, i, 0)),
                   pl.BlockSpec((1, 16, tm), lambda b, i, j: (b, 0, i))],
        out_shape=[jax.ShapeDtypeStruct((B, T, NZ), BF16),
                   jax.ShapeDtypeStruct((B, T, NB), BF16),
                   jax.ShapeDtypeStruct((B, T, 2 * LANES), F32),
                   jax.ShapeDtypeStruct((B, 16, T), F32)],
        scratch_shapes=[pltpu.VMEM((tm + 2 * BF16_ROWS, D), BF16),
                        pltpu.VMEM((qk_w // sub, tm + 2 * BF16_ROWS, sub), F32)],
        compiler_params=_params(3),
        name="in_proj",
    )(h, h, h, cs, sn, w_all, wb_all, wg_all, cw_all, cb_all, qs)


def _na_layout(n_rows):
    n_tiles = n_rows // NA_Q_ROWS
    pairs, plan = [], []
    for gt in (0, 1, n_tiles - 1):
        ws = int(np.clip(NA_Q_ROWS * gt - NA_WIN_H // 2, 0, n_rows - NA_KEY_ROWS))
        per_rq = []
        for rq in range(NA_Q_ROWS):
            r = NA_Q_ROWS * gt + rq
            row_start = int(np.clip(r - NA_WIN_H // 2, 0, n_rows - NA_WIN_H))
            per_pair = []
            for p in range(NA_KEY_ROWS // 2):
                ds = []
                for key_row in (ws + 2 * p, ws + 2 * p + 1):
                    ok = row_start <= key_row < row_start + NA_WIN_H
                    ds.append(key_row - r + NA_WIN_H - 1 if ok else NA_MASKED)
                ds = tuple(ds)
                if ds == (NA_MASKED, NA_MASKED):
                    per_pair.append(None)
                    continue
                if ds not in pairs:
                    pairs.append(ds)
                per_pair.append(pairs.index(ds))
            per_rq.append(per_pair)
        plan.append(per_rq)
    return pairs, plan


def _na_bias_pairs(rpb, pairs):
    L, H = rpb.shape[:2]
    kw = NA_WIN_W
    cq = np.arange(GRID_W)[:, None]
    ck = np.arange(GRID_W)[None, :]
    dc = np.clip(ck - cq, 1 - kw, kw - 1) + kw - 1
    col_start = np.clip(cq - kw // 2, 0, GRID_W - kw)
    ok_c = (ck >= col_start) & (ck < col_start + kw)
    onehot = (dc[..., None] == np.arange(2 * kw - 1)).astype(np.float32)
    toe = jnp.einsum("lhdj,qkj->lhdqk", rpb, jnp.asarray(onehot), precision=lax.Precision.HIGHEST)
    toe = jnp.where(jnp.asarray(ok_c), toe, NEG_BIG)
    toe = jnp.concatenate([toe, jnp.full((L, H, 1, GRID_W, GRID_W), NEG_BIG, F32)], axis=2)
    return jnp.stack([jnp.concatenate([toe[:, :, da], toe[:, :, db]], axis=-1) for da, db in pairs], axis=2)


def _na_kernel(q_ref, k_ref, v_ref, pair_ref, o_ref, *, n_lat, n_lat_tiles, win, hpb, plan):
    g = pl.program_id(2)
    n_pairs = NA_KEY_ROWS // 2

    def latent(tile_plan):
        start = pl.multiple_of(jnp.clip((g - 1) * CHUNK, 0, n_lat - win), CHUNK)
        for hh in range(hpb):
            hs = slice(hh * LANES, (hh + 1) * LANES)
            q = q_ref[0, :, hs]
            vc = v_ref[0, pl.ds(n_lat, CHUNK), hs]
            vl = v_ref[0, pl.ds(start, win), hs]
            s_ctx = _dot_nt(q, k_ref[0, pl.ds(n_lat, CHUNK), hs])
            s_loc = _dot_nt(q, k_ref[0, pl.ds(start, win), hs])
            e_rows, ec_rows, inv_rows = [], [], []
            for rq in range(NA_Q_ROWS):
                rs = slice(rq * GRID_W, (rq + 1) * GRID_W)
                blocks = [None if ci is None else
                          s_loc[rs, p * LANES:(p + 1) * LANES] + pair_ref[0, hh, ci]
                          for p, ci in enumerate(tile_plan[rq])]
                live = [blk for blk in blocks if blk is not None]
                sc = s_ctx[rs]
                top = functools.reduce(jnp.maximum, live + [sc[:, :LANES], sc[:, LANES:]])
                m = jnp.max(top, axis=1, keepdims=True)
                e_blocks = [jnp.zeros((GRID_W, LANES), F32) if blk is None else jnp.exp(blk - m) for blk in blocks]
                e_ctx = jnp.exp(sc - m)
                tot = functools.reduce(jnp.add, [e for e, blk in zip(e_blocks, blocks) if blk is not None]
                                       + [e_ctx[:, :LANES], e_ctx[:, LANES:]])
                inv_rows.append(1.0 / jnp.sum(tot, axis=1, keepdims=True))
                e_rows.append(jnp.concatenate(e_blocks, axis=1).astype(BF16))
                ec_rows.append(e_ctx.astype(BF16))
            o = _dot(jnp.concatenate(e_rows, axis=0), vl) + _dot(jnp.concatenate(ec_rows, axis=0), vc)
            o_ref[0, :, hs] = (o * jnp.concatenate(inv_rows, axis=0)).astype(BF16)

    pl.when(g == 0)(functools.partial(latent, plan[0]))
    pl.when(jnp.logical_and(g > 0, g < n_lat_tiles - 1))(functools.partial(latent, plan[1]))
    pl.when(g == n_lat_tiles - 1)(functools.partial(latent, plan[2]))

    @pl.when(g == n_lat_tiles)
    def _():
        for hh in range(hpb):
            hs = slice(hh * LANES, (hh + 1) * LANES)
            vc = v_ref[0, pl.ds(n_lat, CHUNK), hs]
            s_ctx = _dot_nt(q_ref[0, :, hs], k_ref[0, pl.ds(n_lat, CHUNK), hs])
            e_ctx = jnp.exp(s_ctx - jnp.max(s_ctx, axis=1, keepdims=True))
            denom = jnp.sum(e_ctx, axis=1, keepdims=True)
            o_ref[0, :, hs] = (_dot(e_ctx.astype(BF16), vc) / denom).astype(BF16)


def _neighbourhood_attention(z, pair_all, plan, layer, *, n_lat, n_heads, na_w):
    B, T, _ = z.shape
    n_lat_tiles = n_lat // CHUNK
    win = NA_KEY_ROWS * GRID_W
    hpb = _pick_tile(n_heads, (4, 2, 1))
    n_hg = n_heads // hpb
    n_pair_kinds = pair_all.shape[2]
    kern = functools.partial(_na_kernel, n_lat=n_lat, n_lat_tiles=n_lat_tiles, win=win, hpb=hpb, plan=plan)
    return pl.pallas_call(
        kern,
        grid=(B, n_hg, n_lat_tiles + 1),
        in_specs=[pl.BlockSpec((1, CHUNK, hpb * LANES), lambda b, h, g: (b, g, h)),
                  pl.BlockSpec((1, T, hpb * LANES), lambda b, h, g: (b, 0, n_hg + h)),
                  pl.BlockSpec((1, T, hpb * LANES), lambda b, h, g: (b, 0, 2 * n_hg + h)),
                  pl.BlockSpec((1, hpb, n_pair_kinds, GRID_W, LANES), lambda b, h, g: (layer, h, 0, 0, 0))],
        out_specs=pl.BlockSpec((1, CHUNK, hpb * LANES), lambda b, h, g: (b, g, h)),
        out_shape=jax.ShapeDtypeStruct((B, T, na_w), BF16),
        compiler_params=_params(3),
        name="na_attn",
    )(z, z, z, pair_all)


def _ml_chunk(n, d, n_chunks):
    return jnp.where(n == 0, n_chunks - 1, jnp.where(d == 0, n - 1, n_chunks - 1 - n))


def _mlstm_kernel(qk_ref, v_ref, gcol_ref, grow_ref, gbc_ref, gbr_ref, o_ref, c_scr, m_scr, *, n_heads):
    d = pl.program_id(1)
    n = pl.program_id(2)
    L = CHUNK

    @pl.when(n == 0)
    def _():
        c_scr[...] = jnp.zeros_like(c_scr)
        m_scr[...] = jnp.full_like(m_scr, M_INIT)

    ga = gcol_ref[0] + gbc_ref[0, 0]
    lf_c = _log_sigmoid(ga)
    gr = grow_ref[0] + gbr_ref[0, 0][:, 0:1]
    lf_r = _log_sigmoid(gr)

    sgn = 1 - 2 * d
    ti = lax.broadcasted_iota(jnp.int32, (L, L), 0)
    si = lax.broadcasted_iota(jnp.int32, (L, L), 1)
    mask = ((si - ti) * sgn) <= 0
    maskb = mask.astype(F32).astype(BF16)
    hi, mid, lo = _split3(lf_c)
    b_c = _dot(maskb, hi) + _dot(maskb, mid) + _dot(maskb, lo)
    hi, mid, lo = _split3(jnp.concatenate([lf_r, jnp.zeros_like(lf_r)], axis=0))
    b_r = _dot_nt(hi, maskb) + _dot_nt(mid, maskb) + _dot_nt(lo, maskb)

    ones = jnp.ones((L, LANES), BF16)
    for hh in range(n_heads):
        q = qk_ref[0, :, hh * ML_QK_DIM:(hh + 1) * ML_QK_DIM]
        k = qk_ref[0, :, (n_heads + hh) * ML_QK_DIM:(n_heads + hh + 1) * ML_QK_DIM]
        vext = jnp.concatenate([v_ref[0, :, hh * ML_V_DIM:(hh + 1) * ML_V_DIM], ones], axis=1)
        li_c = ga[:, hh:hh + 1]
        b_col = b_c[:, n_heads + hh:n_heads + hh + 1]
        li_r = gr[hh:hh + 1, :]
        b_row = b_r[n_heads + hh:n_heads + hh + 1, :]
        g_tot = jnp.sum(lf_r[n_heads + hh:n_heads + hh + 1, :], axis=1, keepdims=True)
        m_prev = m_scr[hh, 0:1, 0:1]
        c_prev = c_scr[hh]

        dm = jnp.where(mask, b_col + (li_r - b_row), NEG_BIG)
        inter = b_col + m_prev
        m_t = jnp.maximum(inter, jnp.max(dm, axis=1, keepdims=True))
        w_inter = jnp.exp(inter - m_t)
        p = jnp.exp(dm - m_t)
        s = (_dot_nt(q, k) * p).astype(BF16)
        r = _dot(s, vext) + w_inter * _dot(q, c_prev.astype(BF16))
        den = jnp.maximum(jnp.abs(r[:, ML_V_DIM:ML_V_DIM + 1]), jnp.exp(-m_t))
        o_ref[0, 0, :, hh * ML_V_DIM:(hh + 1) * ML_V_DIM] = (r[:, :ML_V_DIM] / den).astype(BF16)

        a_col = g_tot + li_c - b_col
        ma = jnp.max(a_col, axis=0, keepdims=True)
        wk = (jnp.exp(a_col - ma) * k.astype(F32)).astype(BF16)
        c_loc = _dot_tn(wk, vext)
        m_new = jnp.maximum(g_tot + m_prev, ma)
        dec = jnp.exp(g_tot + m_prev - m_new)
        inc = jnp.exp(ma - m_new)
        c_scr[hh] = dec * c_prev + inc * c_loc
        m_scr[hh] = jnp.broadcast_to(m_new, m_scr.shape[1:])


def _mlstm(z, gcol, grow, gbc_all, gbr_all, layer, *, n_heads, qk_col, v_col):
    B, T, _ = z.shape
    n_chunks = T // CHUNK
    qk_w = 2 * n_heads * ML_QK_DIM
    v_w = n_heads * ML_V_DIM
    qk_blk = qk_col // qk_w
    v_blk = v_col // v_w

    def chunk(n, d):
        return _ml_chunk(n, d, n_chunks)

    kern = functools.partial(_mlstm_kernel, n_heads=n_heads)
    return pl.pallas_call(
        kern,
        grid=(B, 2, n_chunks),
        in_specs=[pl.BlockSpec((1, CHUNK, qk_w), lambda b, d, n: (b, chunk(n, d), qk_blk)),
                  pl.BlockSpec((1, CHUNK, v_w), lambda b, d, n: (b, chunk(n, d), v_blk)),
                  pl.BlockSpec((1, CHUNK, LANES), lambda b, d, n: (b, chunk(n, d), d)),
                  pl.BlockSpec((1, 8, CHUNK), lambda b, d, n: (b, d, chunk(n, d))),
                  pl.BlockSpec((1, 1, 1, LANES), lambda b, d, n: (layer, d, 0, 0)),
                  pl.BlockSpec((1, 1, 8, LANES), lambda b, d, n: (layer, d, 0, 0))],
        out_specs=pl.BlockSpec((1, 1, CHUNK, v_w), lambda b, d, n: (d, b, chunk(n, d), 0)),
        out_shape=jax.ShapeDtypeStruct((2, B, T, v_w), BF16),
        scratch_shapes=[pltpu.VMEM((n_heads, ML_QK_DIM, ML_V_DIM + LANES), F32),
                        pltpu.VMEM((n_heads, 8, LANES), F32)],
        compiler_params=_params(3),
        name="mlstm",
    )(z, z, gcol, grow, gbc_all, gbr_all)


def _merge_kernel(na_ref, hf_ref, hb_ref, mo_ref, gn_ref, gm_ref, x_ref, mod_ref, ng_ref, pg_ref, fg_ref,
                  wna_ref, wml_ref, wo_ref, o_ref, h_ref, *, n_heads, n_tiles, tm, n_lat):
    hs = hf_ref[0, 0].astype(F32) + hb_ref[0, 0].astype(F32)
    parts = []
    for hh in range(n_heads):
        hv = hs[:, hh * ML_V_DIM:(hh + 1) * ML_V_DIM]
        var = jnp.mean(hv * hv, axis=-1, keepdims=True)
        parts.append(hv * lax.rsqrt(var + EPS))
    ml = jnp.concatenate(parts, axis=1) * ng_ref[0] * mo_ref[0].astype(F32)
    y = (gn_ref[0].astype(F32) * _dot(na_ref[0], wna_ref[0])
         + gm_ref[0].astype(F32) * _dot(ml.astype(BF16), wml_ref[0]))
    yo = _dot(y.astype(BF16), wo_ref[0])

    def epilogue(rows, kind):
        x_new = _gated_residual(x_ref[0, rows, :], yo[rows], pg_ref[0], mod_ref[0, 0, kind, 2:3, :])
        o_ref[0, rows, :] = x_new
        h_ref[0, rows, :] = _norm_mod(x_new, fg_ref[0], mod_ref, kind, 3)

    _by_modulation(pl.program_id(1), n_tiles, tm, n_lat, epilogue)


def _merge(na_o, ml_h, z, zb, xc, mods_all, ng_all, pg_all, fg_all, wna_all, wml_all, wo_all, layer, *,
           n_heads, n_lat, mo_col):
    B, T, D = xc.shape
    na_w = na_o.shape[2]
    v_w = ml_h.shape[3]
    tm = _pick_tile(T, (RES_TILE, CHUNK))
    nt = T // tm
    kern = functools.partial(_merge_kernel, n_heads=n_heads, n_tiles=nt, tm=tm, n_lat=n_lat)
    const = dict(pipeline_mode=pl.Buffered(1))
    row = lambda b, i: (b, i, 0)
    vec = lambda b, i: (layer, 0, 0)
    return pl.pallas_call(
        kern,
        grid=(B, nt),
        in_specs=[pl.BlockSpec((1, tm, na_w), row),
                  pl.BlockSpec((1, 1, tm, v_w), lambda b, i: (0, b, i, 0)),
                  pl.BlockSpec((1, 1, tm, v_w), lambda b, i: (1, b, i, 0)),
                  pl.BlockSpec((1, tm, v_w), lambda b, i: (b, i, mo_col // v_w)),
                  pl.BlockSpec((1, tm, D), lambda b, i: (b, i, 0)),
                  pl.BlockSpec((1, tm, D), lambda b, i: (b, i, 1)),
                  pl.BlockSpec((1, tm, D), row),
                  pl.BlockSpec((1, 1, 2, 6, D), lambda b, i: (layer, b, 0, 0, 0)),
                  pl.BlockSpec((1, 1, v_w), vec),
                  pl.BlockSpec((1, 1, D), vec),
                  pl.BlockSpec((1, 1, D), vec),
                  pl.BlockSpec((1, na_w, D), vec, **const),
                  pl.BlockSpec((1, v_w, D), vec, **const),
                  pl.BlockSpec((1, D, D), vec, **const)],
        out_specs=[pl.BlockSpec((1, tm, D), row), pl.BlockSpec((1, tm, D), row)],
        out_shape=[jax.ShapeDtypeStruct((B, T, D), F32), jax.ShapeDtypeStruct((B, T, D), BF16)],
        compiler_params=_params(2),
        name="merge_out_proj",
    )(na_o, ml_h, ml_h, z, zb, zb, xc, mods_all, ng_all, pg_all, fg_all, wna_all, wml_all, wo_all)


def _ffn_sub_blocks(tn):
    return [(c0, min(MXU_COLS, tn - c0)) for c0 in range(0, tn, MXU_COLS)]


def _interleave_gate_value(w_up, tn):
    dff = w_up.shape[2] // 2
    pieces = []
    for t0 in range(0, dff, tn):
        for c0, width in _ffn_sub_blocks(tn):
            pieces += [w_up[:, :, t0 + c0:t0 + c0 + width], w_up[:, :, dff + t0 + c0:dff + t0 + c0 + width]]
    return jnp.concatenate(pieces, axis=2).astype(BF16)


def _ffn_up_kernel(h_ref, hp_ref, hn_ref, w_ref, cw_ref, cb_ref, o_ref, h_scr, u_scr, *, n_lat, n_tot, tm, tn):
    i = pl.program_id(1)
    j = pl.program_id(2)
    H = BF16_ROWS

    @pl.when(j == 0)
    def _():
        _fill_h(h_scr, h_ref, hp_ref, hn_ref, tm)

    is_start, is_end = _seq_edges(i, tm, n_lat, n_tot)
    for sc, (c0, width) in enumerate(_ffn_sub_blocks(tn)):
        cols = slice(c0, c0 + width)
        both = _dot(h_scr[...], w_ref[0, :, 2 * c0:2 * (c0 + width)])
        u = u_scr.at[sc]
        u[:, pl.ds(0, width)] = both[:, :width]
        val = both[H:H + tm, width:]
        y = _conv3(u, width, cw_ref[0, :, cols], cb_ref[0, :, cols], is_start, is_end, tm)
        o_ref[0, :, cols] = (y * _sigmoid(y) * val).astype(BF16)


def _ffn_up(h, w_all, cw_all, cb_all, layer, *, n_lat, tm, tn):
    B, T, D = h.shape
    dff = w_all.shape[2] // 2
    nj = dff // tn
    sub = MXU_COLS
    kern = functools.partial(_ffn_up_kernel, n_lat=n_lat, n_tot=T, tm=tm, tn=tn)
    return pl.pallas_call(
        kern,
        grid=(B, T // tm, nj),
        in_specs=_h_specs(tm, T, D) + [
                  pl.BlockSpec((1, D, 2 * tn), lambda b, i, j: (layer, 0, j)),
                  pl.BlockSpec((1, 3, tn), lambda b, i, j: (layer, 0, j)),
                  pl.BlockSpec((1, 1, tn), lambda b, i, j: (layer, 0, j))],
        out_specs=pl.BlockSpec((1, tm, tn), lambda b, i, j: (b, i, j)),
        out_shape=jax.ShapeDtypeStruct((B, T, dff), BF16),
        scratch_shapes=[pltpu.VMEM((tm + 2 * BF16_ROWS, D), BF16),
                        pltpu.VMEM((pl.cdiv(tn, sub), tm + 2 * BF16_ROWS, sub), F32)],
        compiler_params=_params(3),
        name="ffn_up",
    )(h, h, h, w_all, cw_all, cb_all)


def _ffn_down_kernel(a_ref, x_ref, mod_ref, pg_ref, w_ref, *rest, emit_h, n_tiles, tm, n_lat):
    y = _dot(a_ref[0], w_ref[0])
    if emit_h:
        modn_ref, gn_ref, o_ref, h_ref = rest
    else:
        (o_ref,) = rest

    def epilogue(rows, kind):
        x_new = _gated_residual(x_ref[0, rows, :], y[rows], pg_ref[0], mod_ref[0, 0, kind, 5:6, :])
        o_ref[0, rows, :] = x_new
        if emit_h:
            h_ref[0, rows, :] = _norm_mod(x_new, gn_ref[0], modn_ref, kind, 0)

    _by_modulation(pl.program_id(1), n_tiles, tm, n_lat, epilogue)


def _ffn_down(act, xc, mods_all, pg_all, w_all, gpre_all, layer, *, last, n_lat):
    B, T, D = xc.shape
    dff = act.shape[2]
    tm = CHUNK if last else _pick_tile(T, (RES_TILE, CHUNK))
    nt = T // tm
    row = lambda b, i: (b, i, 0)
    mod = lambda lyr: pl.BlockSpec((1, 1, 2, 6, D), lambda b, i: (lyr, b, 0, 0, 0))
    in_specs = [pl.BlockSpec((1, tm, dff), row),
                pl.BlockSpec((1, tm, D), row),
                mod(layer),
                pl.BlockSpec((1, 1, D), lambda b, i: (layer, 0, 0)),
                pl.BlockSpec((1, dff, D), lambda b, i: (layer, 0, 0), pipeline_mode=pl.Buffered(1))]
    args = [act, xc, mods_all, pg_all, w_all]
    if last:
        assert n_lat % tm == 0
        grid = (B, n_lat // tm)
        out_specs = pl.BlockSpec((1, tm, D), row)
        out_shape = jax.ShapeDtypeStruct((B, n_lat, D), F32)
    else:
        grid = (B, nt)
        in_specs += [mod(layer + 1), pl.BlockSpec((1, 1, D), lambda b, i: (layer + 1, 0, 0))]
        args += [mods_all, gpre_all]
        out_specs = [pl.BlockSpec((1, tm, D), row), pl.BlockSpec((1, tm, D), row)]
        out_shape = [jax.ShapeDtypeStruct((B, T, D), F32), jax.ShapeDtypeStruct((B, T, D), BF16)]
    return pl.pallas_call(
        functools.partial(_ffn_down_kernel, emit_h=not last, n_tiles=nt, tm=tm, n_lat=n_lat),
        grid=grid, in_specs=in_specs, out_specs=out_specs, out_shape=out_shape,
        compiler_params=_params(2),
        name="ffn_down",
    )(*args)


def _rope_tables(n_lat, n_ctx):
    t = np.arange(n_lat)
    row = (t // GRID_W).astype(np.float32)
    col = (t % GRID_W).astype(np.float32)
    n_freq = NA_HEAD_DIM // 4
    inv_freq = jnp.asarray(ROPE_THETA, F32) ** (-jnp.arange(n_freq, dtype=F32) / n_freq)
    ang = jnp.concatenate([jnp.asarray(row)[:, None] * inv_freq, jnp.asarray(col)[:, None] * inv_freq], axis=-1)
    cos, sin = jnp.cos(ang), jnp.sin(ang)
    cs = jnp.concatenate([cos, cos], axis=1)
    sn = jnp.concatenate([-sin, sin], axis=1)
    cs = jnp.concatenate([cs, jnp.ones((n_ctx, NA_HEAD_DIM), F32)], axis=0)
    sn = jnp.concatenate([sn, jnp.zeros((n_ctx, NA_HEAD_DIM), F32)], axis=0)
    return cs, sn


def kernel(x, c, ctx, c_ctx, w_mod, b_mod, norm_mix_pre, norm_mix_post, norm_ffn_pre, norm_ffn_post, w_in, na_rpb, ml_conv_w, ml_conv_b, ml_igate_b, ml_fgate_b, ml_norm_g, w_na_proj, w_ml_proj, w_out, w_up, ffn_conv_w, ffn_conv_b, w_down):
    B, S, D = x.shape
    CL = ctx.shape[1]
    T = S + CL
    L = w_mod.shape[0]
    na_w = w_na_proj.shape[1]
    v_w = w_ml_proj.shape[1]
    n_na_heads = na_w // NA_HEAD_DIM
    n_ml_heads = v_w // ML_V_DIM
    qk_w = 2 * n_ml_heads * ML_QK_DIM
    n_gate = 2 * n_ml_heads
    dff = w_down.shape[1]
    assert CL == CHUNK and S % CHUNK == 0 and S % GRID_W == 0 and S // GRID_W >= NA_KEY_ROWS
    assert n_ml_heads <= 4 and B + 1 <= MOD_ROWS
    gate_col = 3 * na_w + qk_w + 2 * v_w
    assert w_in.shape[2] == gate_col + 2 * n_gate + 2 * D

    qk_col = 3 * na_w
    v_col = qk_col + qk_w
    mo_col = v_col + v_w
    tn = _pick_tile(gate_col, (2048, 1024, 512, 256))
    tn_b = _pick_tile(2 * D, (1024, 512, 256))
    assert all(edge % MXU_COLS == 0 for edge in (na_w, qk_col, v_col, mo_col))
    assert qk_col % qk_w == 0 and v_col % v_w == 0 and mo_col % v_w == 0
    tm = _pick_tile(T, (768, 256))
    tn_ff = _pick_tile(dff, (1408, 512, 256))

    ccol = jnp.pad(jnp.concatenate([c, c_ctx[None]], axis=0).T, ((0, 0), (0, LANES - B - 1)))
    mod_all = _modulation(ccol, w_mod, b_mod, B + 1).reshape(L, MOD_ROWS, 6, D)
    mods_all = jnp.stack([mod_all[:, :B], jnp.broadcast_to(mod_all[:, B:B + 1], (L, B, 6, D))], axis=2)

    cs, sn = _rope_tables(S, CL)
    pairs, plan = _na_layout(S // GRID_W)
    pair_all = _na_bias_pairs(na_rpb, pairs)

    w_main_all = w_in[:, :, :gate_col].astype(BF16)
    w_mgate_all = w_in[:, :, gate_col + 2 * n_gate:].astype(BF16)
    ig = w_in[:, :, gate_col:gate_col + n_gate].reshape(L, D, 2, n_ml_heads)
    fg = w_in[:, :, gate_col + n_gate:gate_col + 2 * n_gate].reshape(L, D, 2, n_ml_heads)
    wg_all = jnp.pad(jnp.concatenate([ig, fg], axis=3), ((0, 0), (0, 0), (0, 0), (0, LANES - n_gate)))
    wg_all = wg_all.reshape(L, D, 2 * LANES).astype(BF16)
    gb = jnp.concatenate([ml_igate_b, ml_fgate_b], axis=2)
    gbc_all = jnp.pad(gb, ((0, 0), (0, 0), (0, LANES - n_gate)))[:, :, None, :]
    gbr_all = jnp.broadcast_to(jnp.pad(gb, ((0, 0), (0, 0), (0, 8 - n_gate)))[..., None], (L, 2, 8, LANES))
    qs = jnp.concatenate([jnp.full((1, qk_w // 2), ML_QK_DIM ** -0.5, F32), jnp.ones((1, qk_w // 2), F32)], axis=1)
    w_na_all, w_ml_all, w_out_all = w_na_proj.astype(BF16), w_ml_proj.astype(BF16), w_out.astype(BF16)
    w_up_all, w_down_all = _interleave_gate_value(w_up, tn_ff), w_down.astype(BF16)
    vec = lambda a: a[:, None, :]

    xc, h = _prenorm(x, ctx, mods_all, vec(norm_mix_pre))
    for l in range(L):
        z, zb, gcol, grow = _in_projection(
            h, cs, sn, w_main_all, w_mgate_all, wg_all, ml_conv_w, vec(ml_conv_b), qs, l,
            n_lat=S, tm=tm, tn=tn, tn_b=tn_b, na_w=na_w, qk_col=qk_col, v_col=v_col, mo_col=mo_col)
        na_o = _neighbourhood_attention(z, pair_all, plan, l, n_lat=S, n_heads=n_na_heads, na_w=na_w)
        ml_h = _mlstm(z, gcol, grow, gbc_all, gbr_all, l, n_heads=n_ml_heads, qk_col=qk_col, v_col=v_col)
        xc, h = _merge(na_o, ml_h, z, zb, xc, mods_all, vec(ml_norm_g), vec(norm_mix_post), vec(norm_ffn_pre),
                       w_na_all, w_ml_all, w_out_all, l, n_heads=n_ml_heads, n_lat=S, mo_col=mo_col)
        act = _ffn_up(h, w_up_all, ffn_conv_w, vec(ffn_conv_b), l, n_lat=S, tm=tm, tn=tn_ff)
        if l == L - 1:
            return _ffn_down(act, xc, mods_all, vec(norm_ffn_post), w_down_all, None, l, last=True, n_lat=S)
        xc, h = _ffn_down(act, xc, mods_all, vec(norm_ffn_post), w_down_all, vec(norm_mix_pre), l, last=False, n_lat=S)
```

```python
import functools

import numpy as np
import jax
import jax.numpy as jnp
from jax import lax
from jax.experimental import pallas as pl
from jax.experimental.pallas import tpu as pltpu

GRID_W = 64
NA_HEAD_DIM = 128
NA_WIN_H = 8
NA_WIN_W = 16
ROPE_THETA = 10000.0
ML_QK_DIM = 128
ML_V_DIM = 256
EPS = 1e-6
M_INIT = -1e30
NEG_BIG = -1e30

LANES = 128
MXU_COLS = 256
BF16_ROWS = 16
MOD_ROWS = 8
CHUNK = 256
RES_TILE = 384
NA_Q_ROWS = CHUNK // GRID_W
NA_KEY_ROWS = 12
NA_MASKED = 2 * NA_WIN_H - 1
VMEM_LIMIT = 56 * 1024 * 1024

F32 = jnp.float32
BF16 = jnp.bfloat16


def _dot(a, b):
    return jnp.dot(a, b, preferred_element_type=F32)


def _dot_nt(a, b):
    return lax.dot_general(a, b, (((1,), (1,)), ((), ())), preferred_element_type=F32)


def _dot_tn(a, b):
    return lax.dot_general(a, b, (((0,), (0,)), ((), ())), preferred_element_type=F32)


def _sigmoid(x):
    return 0.5 * jnp.tanh(0.5 * x) + 0.5


def _log_sigmoid(x):
    return jnp.minimum(x, 0.0) - jnp.log(1.0 + jnp.exp(-jnp.abs(x)))


def _split3(x):
    hi = x.astype(BF16)
    r1 = x - hi.astype(F32)
    mid = r1.astype(BF16)
    lo = (r1 - mid.astype(F32)).astype(BF16)
    return hi, mid, lo


def _params(n_grid):
    return pltpu.CompilerParams(dimension_semantics=("arbitrary",) * n_grid,
                                vmem_limit_bytes=VMEM_LIMIT)


def _pick_tile(n, candidates):
    for c in candidates:
        if n % c == 0:
            return c
    raise ValueError(f"no tile for {n} among {candidates}")


def _norm_mod(x, g, mod_ref, kind, which):
    var = jnp.mean(x * x, axis=-1, keepdims=True)
    y = x * lax.rsqrt(var + EPS) * g
    return (y * (1.0 + mod_ref[0, 0, kind, which + 1:which + 2, :])
            + mod_ref[0, 0, kind, which:which + 1, :]).astype(BF16)


def _by_modulation(i, n_tiles, tm, n_lat, body):
    split = n_lat - (n_tiles - 1) * tm
    assert 0 <= split < tm and split % BF16_ROWS == 0
    pl.when(i < n_tiles - 1)(functools.partial(body, slice(0, tm), 0))

    @pl.when(i == n_tiles - 1)
    def _():
        if split > 0:
            body(slice(0, split), 0)
        body(slice(split, tm), 1)


def _gated_residual(x, y, post_g, gate):
    var = jnp.mean(y * y, axis=-1, keepdims=True)
    return x + gate * (y * lax.rsqrt(var + EPS) * post_g)


def _mod_kernel(ct_ref, w_ref, b_ref, o_ref, *, n_rows):
    ct = ct_ref[...]
    s = ct * _sigmoid(ct)
    w = w_ref[0]
    rows = [jnp.sum(w * s[:, r:r + 1], axis=0, keepdims=True) for r in range(n_rows)]
    rows.append(jnp.zeros((MOD_ROWS - n_rows, w.shape[1]), F32))
    o_ref[0] = jnp.concatenate(rows, axis=0) + b_ref[0]


def _modulation(ccol, w_mod, b_mod, n_rows):
    L, D, N = w_mod.shape
    tn = _pick_tile(N, (1024, 512, 256, 128))
    return pl.pallas_call(
        functools.partial(_mod_kernel, n_rows=n_rows),
        grid=(L, N // tn),
        in_specs=[pl.BlockSpec((D, LANES), lambda l, j: (0, 0)),
                  pl.BlockSpec((1, D, tn), lambda l, j: (l, 0, j)),
                  pl.BlockSpec((1, 1, tn), lambda l, j: (l, 0, j))],
        out_specs=pl.BlockSpec((1, MOD_ROWS, tn), lambda l, j: (l, 0, j)),
        out_shape=jax.ShapeDtypeStruct((L, MOD_ROWS, N), F32),
        compiler_params=_params(2),
        name="adaln_mod",
    )(ccol, w_mod, b_mod.reshape(L, 1, N))


def _prenorm_kernel(x_ref, ctx_ref, mod_ref, g_ref, xc_ref, h_ref, *, n_lat_tiles):
    def emit(src_ref):
        xc_ref[0] = src_ref[0]
        h_ref[0] = _norm_mod(src_ref[0], g_ref[0], mod_ref, 0, 0)

    i = pl.program_id(1)
    pl.when(i < n_lat_tiles)(functools.partial(emit, x_ref))
    pl.when(i == n_lat_tiles)(functools.partial(emit, ctx_ref))


def _prenorm(x, ctx, mods_all, g_all):
    B, S, D = x.shape
    T = S + ctx.shape[1]
    nt = T // CHUNK
    row = lambda b, i: (b, i, 0)
    return pl.pallas_call(
        functools.partial(_prenorm_kernel, n_lat_tiles=nt - 1),
        grid=(B, nt),
        in_specs=[pl.BlockSpec((1, CHUNK, D), lambda b, i: (b, jnp.minimum(i, nt - 2), 0)),
                  pl.BlockSpec((1, CHUNK, D), lambda b, i: (b, 0, 0)),
                  pl.BlockSpec((1, 1, 1, 6, D), lambda b, i: (0, b, i // (nt - 1), 0, 0)),
                  pl.BlockSpec((1, 1, D), lambda b, i: (0, 0, 0))],
        out_specs=[pl.BlockSpec((1, CHUNK, D), row), pl.BlockSpec((1, CHUNK, D), row)],
        out_shape=[jax.ShapeDtypeStruct((B, T, D), F32), jax.ShapeDtypeStruct((B, T, D), BF16)],
        compiler_params=_params(2),
        name="prenorm",
    )(x, ctx, mods_all, g_all)


def _fill_h(h_scr, h_ref, hp_ref, hn_ref, tm):
    H = BF16_ROWS
    h_scr[pl.ds(0, H), :] = hp_ref[0]
    h_scr[pl.ds(H, tm), :] = h_ref[0]
    h_scr[pl.ds(H + tm, H), :] = hn_ref[0]


def _seq_edges(i, tm, n_lat, n_tot):
    row = i * tm + lax.broadcasted_iota(jnp.int32, (tm, 1), 0)
    is_start = jnp.logical_or(row == 0, row == n_lat)
    is_end = jnp.logical_or(row == n_lat - 1, row == n_tot - 1)
    return is_start, is_end


def _conv3(u_ref, width, cw, cb, is_start, is_end, tm):
    H = BF16_ROWS
    u_prev = jnp.where(is_start, 0.0, u_ref[pl.ds(H - 1, tm), pl.ds(0, width)])
    u_next = jnp.where(is_end, 0.0, u_ref[pl.ds(H + 1, tm), pl.ds(0, width)])
    return cw[0:1, :] * u_prev + cw[1:2, :] * u_ref[pl.ds(H, tm), pl.ds(0, width)] + cw[2:3, :] * u_next + cb


def _h_specs(tm, T, D):
    hb = tm // BF16_ROWS
    n_hblk = T // BF16_ROWS
    return [pl.BlockSpec((1, tm, D), lambda b, i, j: (b, i, 0)),
            pl.BlockSpec((1, BF16_ROWS, D), lambda b, i, j: (b, jnp.maximum(i * hb - 1, 0), 0)),
            pl.BlockSpec((1, BF16_ROWS, D), lambda b, i, j: (b, jnp.minimum((i + 1) * hb, n_hblk - 1), 0))]


def _inproj_kernel(h_ref, hp_ref, hn_ref, cs_ref, sn_ref, w_ref, wg_ref, cw_ref, cb_ref, qs_ref,
                   z_ref, gcol_ref, grow_ref, h_scr, u_scr, *,
                   n_lat, n_tot, tm, tn, n_col, sub, na_w, qk_col, v_col, mo_col, q_scale):
    i = pl.program_id(1)
    j = pl.program_id(2)

    @pl.when(j == 0)
    def _():
        _fill_h(h_scr, h_ref, hp_ref, hn_ref, tm)
        gates = _dot(h_ref[0], wg_ref[0])
        gcol_ref[0] = gates
        grow_ref[0] = jnp.concatenate([gates[:, :LANES].T[0:8], gates[:, LANES:].T[0:8]], axis=0)

    def col_tile(jj):
        is_start, is_end = _seq_edges(i, tm, n_lat, n_tot)
        for sc in range(tn // sub):
            col = jj * tn + sc * sub
            cols = slice(sc * sub, (sc + 1) * sub)
            if qk_col <= col < v_col:
                u = u_scr.at[(col - qk_col) // sub]
                u[...] = _dot(h_scr[...], w_ref[0, :, cols])
                cc = slice(col - qk_col, col - qk_col + sub)
                y = _conv3(u, sub, cw_ref[0, :, cc], cb_ref[0, :, cc], is_start, is_end, tm)
                z_ref[0, :, cols] = (y * _sigmoid(y) * qs_ref[:, cc]).astype(BF16)
                continue
            acc = _dot(h_ref[0], w_ref[0, :, cols])
            if col < 2 * na_w:
                for c in range(sub // LANES):
                    chunk = acc[:, c * LANES:(c + 1) * LANES]
                    out = chunk * cs_ref[...] + pltpu.roll(chunk, LANES // 2, axis=1) * sn_ref[...]
                    if col < na_w:
                        out = out * q_scale
                    z_ref[0, :, sc * sub + c * LANES:sc * sub + (c + 1) * LANES] = out.astype(BF16)
            elif col < mo_col:
                z_ref[0, :, cols] = acc.astype(BF16)
            else:
                z_ref[0, :, cols] = _sigmoid(acc).astype(BF16)

    for jj in range(n_col):
        pl.when(j == jj)(functools.partial(col_tile, jj))


def _in_projection(h, cs, sn, w_all, wg_all, cw_all, cb_all, qs, layer, *, n_lat, tm, tn, na_w, qk_col, v_col, mo_col):
    B, T, D = h.shape
    NZ = w_all.shape[2]
    sub = min(tn, MXU_COLS)
    qk_w = v_col - qk_col
    kern = functools.partial(_inproj_kernel, n_lat=n_lat, n_tot=T, tm=tm, tn=tn, n_col=NZ // tn, sub=sub, na_w=na_w,
                             qk_col=qk_col, v_col=v_col, mo_col=mo_col, q_scale=NA_HEAD_DIM ** -0.5)
    return pl.pallas_call(
        kern,
        grid=(B, T // tm, NZ // tn),
        in_specs=_h_specs(tm, T, D) + [
                  pl.BlockSpec((tm, LANES), lambda b, i, j: (i, 0)),
                  pl.BlockSpec((tm, LANES), lambda b, i, j: (i, 0)),
                  pl.BlockSpec((1, D, tn), lambda b, i, j: (layer, 0, j)),
                  pl.BlockSpec((1, D, 2 * LANES), lambda b, i, j: (layer, 0, 0)),
                  pl.BlockSpec((1, 3, qk_w), lambda b, i, j: (layer, 0, 0)),
                  pl.BlockSpec((1, 1, qk_w), lambda b, i, j: (layer, 0, 0)),
                  pl.BlockSpec((1, qk_w), lambda b, i, j: (0, 0))],
        out_specs=[pl.BlockSpec((1, tm, tn), lambda b, i, j: (b, i, j)),
                   pl.BlockSpec((1, tm, 2 * LANES), lambda b, i, j: (b, i, 0)),
                   pl.BlockSpec((1, 16, tm), lambda b, i, j: (b, 0, i))],
        out_shape=[jax.ShapeDtypeStruct((B, T, NZ), BF16),
                   jax.ShapeDtypeStruct((B, T, 2 * LANES), F32),
                   jax.ShapeDtypeStruct((B, 16, T), F32)],
        scratch_shapes=[pltpu.VMEM((tm + 2 * BF16_ROWS, D), BF16),
                        pltpu.VMEM((qk_w // sub, tm + 2 * BF16_ROWS, sub), F32)],
        compiler_params=_params(3),
        name="in_proj",
    )(h, h, h, cs, sn, w_all, wg_all, cw_all, cb_all, qs)


def _na_layout(n_rows):
    n_tiles = n_rows // NA_Q_ROWS
    pairs, plan = [], []
    for gt in (0, 1, n_tiles - 1):
        ws = int(np.clip(NA_Q_ROWS * gt - NA_WIN_H // 2, 0, n_rows - NA_KEY_ROWS))
        per_rq = []
        for rq in range(NA_Q_ROWS):
            r = NA_Q_ROWS * gt + rq
            row_start = int(np.clip(r - NA_WIN_H // 2, 0, n_rows - NA_WIN_H))
            per_pair = []
            for p in range(NA_KEY_ROWS // 2):
                ds = []
                for key_row in (ws + 2 * p, ws + 2 * p + 1):
                    ok = row_start <= key_row < row_start + NA_WIN_H
                    ds.append(key_row - r + NA_WIN_H - 1 if ok else NA_MASKED)
                ds = tuple(ds)
                if ds == (NA_MASKED, NA_MASKED):
                    per_pair.append(None)
                    continue
                if ds not in pairs:
                    pairs.append(ds)
                per_pair.append(pairs.index(ds))
            per_rq.append(per_pair)
        plan.append(per_rq)
    return pairs, plan


def _na_bias_pairs(rpb, pairs):
    L, H = rpb.shape[:2]
    kw = NA_WIN_W
    cq = np.arange(GRID_W)[:, None]
    ck = np.arange(GRID_W)[None, :]
    dc = np.clip(ck - cq, 1 - kw, kw - 1) + kw - 1
    col_start = np.clip(cq - kw // 2, 0, GRID_W - kw)
    ok_c = (ck >= col_start) & (ck < col_start + kw)
    onehot = (dc[..., None] == np.arange(2 * kw - 1)).astype(np.float32)
    toe = jnp.einsum("lhdj,qkj->lhdqk", rpb, jnp.asarray(onehot), precision=lax.Precision.HIGHEST)
    toe = jnp.where(jnp.asarray(ok_c), toe, NEG_BIG)
    toe = jnp.concatenate([toe, jnp.full((L, H, 1, GRID_W, GRID_W), NEG_BIG, F32)], axis=2)
    return jnp.stack([jnp.concatenate([toe[:, :, da], toe[:, :, db]], axis=-1) for da, db in pairs], axis=2)


def _na_kernel(q_ref, k_ref, v_ref, pair_ref, o_ref, *, n_lat, n_lat_tiles, win, hpb, plan):
    g = pl.program_id(2)
    n_pairs = NA_KEY_ROWS // 2

    def latent(tile_plan):
        start = pl.multiple_of(jnp.clip((g - 1) * CHUNK, 0, n_lat - win), CHUNK)
        for hh in range(hpb):
            hs = slice(hh * LANES, (hh + 1) * LANES)
            q = q_ref[0, :, hs]
            vc = v_ref[0, pl.ds(n_lat, CHUNK), hs]
            vl = v_ref[0, pl.ds(start, win), hs]
            s_ctx = _dot_nt(q, k_ref[0, pl.ds(n_lat, CHUNK), hs])
            s_loc = _dot_nt(q, k_ref[0, pl.ds(start, win), hs])
            e_rows, ec_rows, inv_rows = [], [], []
            for rq in range(NA_Q_ROWS):
                rs = slice(rq * GRID_W, (rq + 1) * GRID_W)
                blocks = [None if ci is None else
                          s_loc[rs, p * LANES:(p + 1) * LANES] + pair_ref[0, hh, ci]
                          for p, ci in enumerate(tile_plan[rq])]
                live = [blk for blk in blocks if blk is not None]
                sc = s_ctx[rs]
                top = functools.reduce(jnp.maximum, live + [sc[:, :LANES], sc[:, LANES:]])
                m = jnp.max(top, axis=1, keepdims=True)
                e_blocks = [jnp.zeros((GRID_W, LANES), F32) if blk is None else jnp.exp(blk - m) for blk in blocks]
                e_ctx = jnp.exp(sc - m)
                tot = functools.reduce(jnp.add, [e for e, blk in zip(e_blocks, blocks) if blk is not None]
                                       + [e_ctx[:, :LANES], e_ctx[:, LANES:]])
                inv_rows.append(1.0 / jnp.sum(tot, axis=1, keepdims=True))
                e_rows.append(jnp.concatenate(e_blocks, axis=1).astype(BF16))
                ec_rows.append(e_ctx.astype(BF16))
            o = _dot(jnp.concatenate(e_rows, axis=0), vl) + _dot(jnp.concatenate(ec_rows, axis=0), vc)
            o_ref[0, :, hs] = (o * jnp.concatenate(inv_rows, axis=0)).astype(BF16)

    pl.when(g == 0)(functools.partial(latent, plan[0]))
    pl.when(jnp.logical_and(g > 0, g < n_lat_tiles - 1))(functools.partial(latent, plan[1]))
    pl.when(g == n_lat_tiles - 1)(functools.partial(latent, plan[2]))

    @pl.when(g == n_lat_tiles)
    def _():
        for hh in range(hpb):
            hs = slice(hh * LANES, (hh + 1) * LANES)
            vc = v_ref[0, pl.ds(n_lat, CHUNK), hs]
            s_ctx = _dot_nt(q_ref[0, :, hs], k_ref[0, pl.ds(n_lat, CHUNK), hs])
            e_ctx = jnp.exp(s_ctx - jnp.max(s_ctx, axis=1, keepdims=True))
            denom = jnp.sum(e_ctx, axis=1, keepdims=True)
            o_ref[0, :, hs] = (_dot(e_ctx.astype(BF16), vc) / denom).astype(BF16)


def _neighbourhood_attention(z, pair_all, plan, layer, *, n_lat, n_heads, na_w):
    B, T, _ = z.shape
    n_lat_tiles = n_lat // CHUNK
    win = NA_KEY_ROWS * GRID_W
    hpb = _pick_tile(n_heads, (4, 2, 1))
    n_hg = n_heads // hpb
    n_pair_kinds = pair_all.shape[2]
    kern = functools.partial(_na_kernel, n_lat=n_lat, n_lat_tiles=n_lat_tiles, win=win, hpb=hpb, plan=plan)
    return pl.pallas_call(
        kern,
        grid=(B, n_hg, n_lat_tiles + 1),
        in_specs=[pl.BlockSpec((1, CHUNK, hpb * LANES), lambda b, h, g: (b, g, h)),
                  pl.BlockSpec((1, T, hpb * LANES), lambda b, h, g: (b, 0, n_hg + h)),
                  pl.BlockSpec((1, T, hpb * LANES), lambda b, h, g: (b, 0, 2 * n_hg + h)),
                  pl.BlockSpec((1, hpb, n_pair_kinds, GRID_W, LANES), lambda b, h, g: (layer, h, 0, 0, 0))],
        out_specs=pl.BlockSpec((1, CHUNK, hpb * LANES), lambda b, h, g: (b, g, h)),
        out_shape=jax.ShapeDtypeStruct((B, T, na_w), BF16),
        compiler_params=_params(3),
        name="na_attn",
    )(z, z, z, pair_all)


def _ml_chunk(n, d, n_chunks):
    return jnp.where(n == 0, n_chunks - 1, jnp.where(d == 0, n - 1, n_chunks - 1 - n))


def _mlstm_kernel(qk_ref, v_ref, gcol_ref, grow_ref, gbc_ref, gbr_ref, o_ref, c_scr, m_scr, *, n_heads):
    d = pl.program_id(1)
    n = pl.program_id(2)
    L = CHUNK

    @pl.when(n == 0)
    def _():
        c_scr[...] = jnp.zeros_like(c_scr)
        m_scr[...] = jnp.full_like(m_scr, M_INIT)

    ga = gcol_ref[0] + gbc_ref[0, 0]
    lf_c = _log_sigmoid(ga)
    gr = grow_ref[0] + gbr_ref[0, 0][:, 0:1]
    lf_r = _log_sigmoid(gr)

    sgn = 1 - 2 * d
    ti = lax.broadcasted_iota(jnp.int32, (L, L), 0)
    si = lax.broadcasted_iota(jnp.int32, (L, L), 1)
    mask = ((si - ti) * sgn) <= 0
    maskb = mask.astype(F32).astype(BF16)
    hi, mid, lo = _split3(lf_c)
    b_c = _dot(maskb, hi) + _dot(maskb, mid) + _dot(maskb, lo)
    hi, mid, lo = _split3(jnp.concatenate([lf_r, jnp.zeros_like(lf_r)], axis=0))
    b_r = _dot_nt(hi, maskb) + _dot_nt(mid, maskb) + _dot_nt(lo, maskb)

    ones = jnp.ones((L, LANES), BF16)
    for hh in range(n_heads):
        q = qk_ref[0, :, hh * ML_QK_DIM:(hh + 1) * ML_QK_DIM]
        k = qk_ref[0, :, (n_heads + hh) * ML_QK_DIM:(n_heads + hh + 1) * ML_QK_DIM]
        vext = jnp.concatenate([v_ref[0, :, hh * ML_V_DIM:(hh + 1) * ML_V_DIM], ones], axis=1)
        li_c = ga[:, hh:hh + 1]
        b_col = b_c[:, n_heads + hh:n_heads + hh + 1]
        li_r = gr[hh:hh + 1, :]
        b_row = b_r[n_heads + hh:n_heads + hh + 1, :]
        g_tot = jnp.sum(lf_r[n_heads + hh:n_heads + hh + 1, :], axis=1, keepdims=True)
        m_prev = m_scr[hh, 0:1, 0:1]
        c_prev = c_scr[hh]

        dm = jnp.where(mask, b_col + (li_r - b_row), NEG_BIG)
        inter = b_col + m_prev
        m_t = jnp.maximum(inter, jnp.max(dm, axis=1, keepdims=True))
        w_inter = jnp.exp(inter - m_t)
        p = jnp.exp(dm - m_t)
        s = (_dot_nt(q, k) * p).astype(BF16)
        r = _dot(s, vext) + w_inter * _dot(q, c_prev.astype(BF16))
        den = jnp.maximum(jnp.abs(r[:, ML_V_DIM:ML_V_DIM + 1]), jnp.exp(-m_t))
        o_ref[0, 0, :, hh * ML_V_DIM:(hh + 1) * ML_V_DIM] = (r[:, :ML_V_DIM] / den).astype(BF16)

        a_col = g_tot + li_c - b_col
        ma = jnp.max(a_col, axis=0, keepdims=True)
        wk = (jnp.exp(a_col - ma) * k.astype(F32)).astype(BF16)
        c_loc = _dot_tn(wk, vext)
        m_new = jnp.maximum(g_tot + m_prev, ma)
        dec = jnp.exp(g_tot + m_prev - m_new)
        inc = jnp.exp(ma - m_new)
        c_scr[hh] = dec * c_prev + inc * c_loc
        m_scr[hh] = jnp.broadcast_to(m_new, m_scr.shape[1:])


def _mlstm(z, gcol, grow, gbc_all, gbr_all, layer, *, n_heads, qk_col, v_col):
    B, T, _ = z.shape
    n_chunks = T // CHUNK
    qk_w = 2 * n_heads * ML_QK_DIM
    v_w = n_heads * ML_V_DIM
    qk_blk = qk_col // qk_w
    v_blk = v_col // v_w

    def chunk(n, d):
        return _ml_chunk(n, d, n_chunks)

    kern = functools.partial(_mlstm_kernel, n_heads=n_heads)
    return pl.pallas_call(
        kern,
        grid=(B, 2, n_chunks),
        in_specs=[pl.BlockSpec((1, CHUNK, qk_w), lambda b, d, n: (b, chunk(n, d), qk_blk)),
                  pl.BlockSpec((1, CHUNK, v_w), lambda b, d, n: (b, chunk(n, d), v_blk)),
                  pl.BlockSpec((1, CHUNK, LANES), lambda b, d, n: (b, chunk(n, d), d)),
                  pl.BlockSpec((1, 8, CHUNK), lambda b, d, n: (b, d, chunk(n, d))),
                  pl.BlockSpec((1, 1, 1, LANES), lambda b, d, n: (layer, d, 0, 0)),
                  pl.BlockSpec((1, 1, 8, LANES), lambda b, d, n: (layer, d, 0, 0))],
        out_specs=pl.BlockSpec((1, 1, CHUNK, v_w), lambda b, d, n: (d, b, chunk(n, d), 0)),
        out_shape=jax.ShapeDtypeStruct((2, B, T, v_w), BF16),
        scratch_shapes=[pltpu.VMEM((n_heads, ML_QK_DIM, ML_V_DIM + LANES), F32),
                        pltpu.VMEM((n_heads, 8, LANES), F32)],
        compiler_params=_params(3),
        name="mlstm",
    )(z, z, gcol, grow, gbc_all, gbr_all)


def _merge_kernel(na_ref, hf_ref, hb_ref, mo_ref, gn_ref, gm_ref, x_ref, mod_ref, ng_ref, pg_ref, fg_ref,
                  wna_ref, wml_ref, wo_ref, o_ref, h_ref, *, n_heads, n_tiles, tm, n_lat):
    hs = hf_ref[0, 0].astype(F32) + hb_ref[0, 0].astype(F32)
    parts = []
    for hh in range(n_heads):
        hv = hs[:, hh * ML_V_DIM:(hh + 1) * ML_V_DIM]
        var = jnp.mean(hv * hv, axis=-1, keepdims=True)
        parts.append(hv * lax.rsqrt(var + EPS))
    ml = jnp.concatenate(parts, axis=1) * ng_ref[0] * mo_ref[0].astype(F32)
    y = (gn_ref[0].astype(F32) * _dot(na_ref[0], wna_ref[0])
         + gm_ref[0].astype(F32) * _dot(ml.astype(BF16), wml_ref[0]))
    yo = _dot(y.astype(BF16), wo_ref[0])

    def epilogue(rows, kind):
        x_new = _gated_residual(x_ref[0, rows, :], yo[rows], pg_ref[0], mod_ref[0, 0, kind, 2:3, :])
        o_ref[0, rows, :] = x_new
        h_ref[0, rows, :] = _norm_mod(x_new, fg_ref[0], mod_ref, kind, 3)

    _by_modulation(pl.program_id(1), n_tiles, tm, n_lat, epilogue)


def _merge(na_o, ml_h, z, xc, mods_all, ng_all, pg_all, fg_all, wna_all, wml_all, wo_all, layer, *,
           n_heads, n_lat, mo_col, gn_col, gm_col):
    B, T, D = xc.shape
    na_w = na_o.shape[2]
    v_w = ml_h.shape[3]
    tm = _pick_tile(T, (RES_TILE, CHUNK))
    nt = T // tm
    kern = functools.partial(_merge_kernel, n_heads=n_heads, n_tiles=nt, tm=tm, n_lat=n_lat)
    const = dict(pipeline_mode=pl.Buffered(1))
    row = lambda b, i: (b, i, 0)
    vec = lambda b, i: (layer, 0, 0)
    return pl.pallas_call(
        kern,
        grid=(B, nt),
        in_specs=[pl.BlockSpec((1, tm, na_w), row),
                  pl.BlockSpec((1, 1, tm, v_w), lambda b, i: (0, b, i, 0)),
                  pl.BlockSpec((1, 1, tm, v_w), lambda b, i: (1, b, i, 0)),
                  pl.BlockSpec((1, tm, v_w), lambda b, i: (b, i, mo_col // v_w)),
                  pl.BlockSpec((1, tm, D), lambda b, i: (b, i, gn_col // D)),
                  pl.BlockSpec((1, tm, D), lambda b, i: (b, i, gm_col // D)),
                  pl.BlockSpec((1, tm, D), row),
                  pl.BlockSpec((1, 1, 2, 6, D), lambda b, i: (layer, b, 0, 0, 0)),
                  pl.BlockSpec((1, 1, v_w), vec),
                  pl.BlockSpec((1, 1, D), vec),
                  pl.BlockSpec((1, 1, D), vec),
                  pl.BlockSpec((1, na_w, D), vec, **const),
                  pl.BlockSpec((1, v_w, D), vec, **const),
                  pl.BlockSpec((1, D, D), vec, **const)],
        out_specs=[pl.BlockSpec((1, tm, D), row), pl.BlockSpec((1, tm, D), row)],
        out_shape=[jax.ShapeDtypeStruct((B, T, D), F32), jax.ShapeDtypeStruct((B, T, D), BF16)],
        compiler_params=_params(2),
        name="merge_out_proj",
    )(na_o, ml_h, ml_h, z, z, z, xc, mods_all, ng_all, pg_all, fg_all, wna_all, wml_all, wo_all)


def _ffn_sub_blocks(tn):
    return [(c0, min(MXU_COLS, tn - c0)) for c0 in range(0, tn, MXU_COLS)]


def _ffn_up_kernel(h_ref, hp_ref, hn_ref, wg_ref, wv_ref, cw_ref, cb_ref, o_ref, h_scr, u_scr, *,
                   n_lat, n_tot, tm, tn):
    i = pl.program_id(1)
    j = pl.program_id(2)
    H = BF16_ROWS

    @pl.when(j == 0)
    def _():
        _fill_h(h_scr, h_ref, hp_ref, hn_ref, tm)

    is_start, is_end = _seq_edges(i, tm, n_lat, n_tot)
    for sc, (c0, width) in enumerate(_ffn_sub_blocks(tn)):
        cols = slice(c0, c0 + width)
        both = _dot(h_scr[...], jnp.concatenate([wg_ref[0, :, cols], wv_ref[0, :, cols]], axis=1))
        u = u_scr.at[sc]
        u[:, pl.ds(0, width)] = both[:, :width]
        val = both[H:H + tm, width:]
        y = _conv3(u, width, cw_ref[0, :, cols], cb_ref[0, :, cols], is_start, is_end, tm)
        o_ref[0, :, cols] = (y * _sigmoid(y) * val).astype(BF16)


def _ffn_up(h, w_all, cw_all, cb_all, layer, *, n_lat, tm, tn):
    B, T, D = h.shape
    dff = w_all.shape[2] // 2
    nj = dff // tn
    sub = MXU_COLS
    kern = functools.partial(_ffn_up_kernel, n_lat=n_lat, n_tot=T, tm=tm, tn=tn)
    return pl.pallas_call(
        kern,
        grid=(B, T // tm, nj),
        in_specs=_h_specs(tm, T, D) + [
                  pl.BlockSpec((1, D, tn), lambda b, i, j: (layer, 0, j)),
                  pl.BlockSpec((1, D, tn), lambda b, i, j: (layer, 0, nj + j)),
                  pl.BlockSpec((1, 3, tn), lambda b, i, j: (layer, 0, j)),
                  pl.BlockSpec((1, 1, tn), lambda b, i, j: (layer, 0, j))],
        out_specs=pl.BlockSpec((1, tm, tn), lambda b, i, j: (b, i, j)),
        out_shape=jax.ShapeDtypeStruct((B, T, dff), BF16),
        scratch_shapes=[pltpu.VMEM((tm + 2 * BF16_ROWS, D), BF16),
                        pltpu.VMEM((pl.cdiv(tn, sub), tm + 2 * BF16_ROWS, sub), F32)],
        compiler_params=_params(3),
        name="ffn_up",
    )(h, h, h, w_all, w_all, cw_all, cb_all)


def _ffn_down_kernel(a_ref, x_ref, mod_ref, pg_ref, w_ref, *rest, emit_h, n_tiles, tm, n_lat):
    y = _dot(a_ref[0], w_ref[0])
    if emit_h:
        modn_ref, gn_ref, o_ref, h_ref = rest
    else:
        (o_ref,) = rest

    def epilogue(rows, kind):
        x_new = _gated_residual(x_ref[0, rows, :], y[rows], pg_ref[0], mod_ref[0, 0, kind, 5:6, :])
        o_ref[0, rows, :] = x_new
        if emit_h:
            h_ref[0, rows, :] = _norm_mod(x_new, gn_ref[0], modn_ref, kind, 0)

    _by_modulation(pl.program_id(1), n_tiles, tm, n_lat, epilogue)


def _ffn_down(act, xc, mods_all, pg_all, w_all, gpre_all, layer, *, last, n_lat):
    B, T, D = xc.shape
    dff = act.shape[2]
    tm = CHUNK if last else _pick_tile(T, (RES_TILE, CHUNK))
    nt = T // tm
    row = lambda b, i: (b, i, 0)
    mod = lambda lyr: pl.BlockSpec((1, 1, 2, 6, D), lambda b, i: (lyr, b, 0, 0, 0))
    in_specs = [pl.BlockSpec((1, tm, dff), row),
                pl.BlockSpec((1, tm, D), row),
                mod(layer),
                pl.BlockSpec((1, 1, D), lambda b, i: (layer, 0, 0)),
                pl.BlockSpec((1, dff, D), lambda b, i: (layer, 0, 0), pipeline_mode=pl.Buffered(1))]
    args = [act, xc, mods_all, pg_all, w_all]
    if last:
        assert n_lat % tm == 0
        grid = (B, n_lat // tm)
        out_specs = pl.BlockSpec((1, tm, D), row)
        out_shape = jax.ShapeDtypeStruct((B, n_lat, D), F32)
    else:
        grid = (B, nt)
        in_specs += [mod(layer + 1), pl.BlockSpec((1, 1, D), lambda b, i: (layer + 1, 0, 0))]
        args += [mods_all, gpre_all]
        out_specs = [pl.BlockSpec((1, tm, D), row), pl.BlockSpec((1, tm, D), row)]
        out_shape = [jax.ShapeDtypeStruct((B, T, D), F32), jax.ShapeDtypeStruct((B, T, D), BF16)]
    return pl.pallas_call(
        functools.partial(_ffn_down_kernel, emit_h=not last, n_tiles=nt, tm=tm, n_lat=n_lat),
        grid=grid, in_specs=in_specs, out_specs=out_specs, out_shape=out_shape,
        compiler_params=_params(2),
        name="ffn_down",
    )(*args)


def _rope_tables(n_lat, n_ctx):
    t = np.arange(n_lat)
    row = (t // GRID_W).astype(np.float32)
    col = (t % GRID_W).astype(np.float32)
    n_freq = NA_HEAD_DIM // 4
    inv_freq = jnp.asarray(ROPE_THETA, F32) ** (-jnp.arange(n_freq, dtype=F32) / n_freq)
    ang = jnp.concatenate([jnp.asarray(row)[:, None] * inv_freq, jnp.asarray(col)[:, None] * inv_freq], axis=-1)
    cos, sin = jnp.cos(ang), jnp.sin(ang)
    cs = jnp.concatenate([cos, cos], axis=1)
    sn = jnp.concatenate([-sin, sin], axis=1)
    cs = jnp.concatenate([cs, jnp.ones((n_ctx, NA_HEAD_DIM), F32)], axis=0)
    sn = jnp.concatenate([sn, jnp.zeros((n_ctx, NA_HEAD_DIM), F32)], axis=0)
    return cs, sn


def kernel(x, c, ctx, c_ctx, w_mod, b_mod, norm_mix_pre, norm_mix_post, norm_ffn_pre, norm_ffn_post, w_in, na_rpb, ml_conv_w, ml_conv_b, ml_igate_b, ml_fgate_b, ml_norm_g, w_na_proj, w_ml_proj, w_out, w_up, ffn_conv_w, ffn_conv_b, w_down):
    B, S, D = x.shape
    CL = ctx.shape[1]
    T = S + CL
    L = w_mod.shape[0]
    na_w = w_na_proj.shape[1]
    v_w = w_ml_proj.shape[1]
    n_na_heads = na_w // NA_HEAD_DIM
    n_ml_heads = v_w // ML_V_DIM
    qk_w = 2 * n_ml_heads * ML_QK_DIM
    n_gate = 2 * n_ml_heads
    dff = w_down.shape[1]
    assert CL == CHUNK and S % CHUNK == 0 and S % GRID_W == 0 and S // GRID_W >= NA_KEY_ROWS
    assert n_ml_heads <= 4 and B + 1 <= MOD_ROWS
    gate_col = 3 * na_w + qk_w + 2 * v_w
    assert w_in.shape[2] == gate_col + 2 * n_gate + 2 * D

    qk_col = 3 * na_w
    v_col = qk_col + qk_w
    mo_col = v_col + v_w
    gn_col = mo_col + v_w
    gm_col = gn_col + D
    tn = _pick_tile(gm_col + D, (2048, 1024, 512, 256))
    assert all(edge % MXU_COLS == 0 for edge in (na_w, qk_col, v_col, mo_col))
    assert qk_col % qk_w == 0 and v_col % v_w == 0 and mo_col % v_w == 0 and gn_col % D == 0
    tm = _pick_tile(T, (768, 256))
    tn_ff = _pick_tile(dff, (1408, 512, 256))

    ccol = jnp.pad(jnp.concatenate([c, c_ctx[None]], axis=0).T, ((0, 0), (0, LANES - B - 1)))
    mod_all = _modulation(ccol, w_mod, b_mod, B + 1).reshape(L, MOD_ROWS, 6, D)
    mods_all = jnp.stack([mod_all[:, :B], jnp.broadcast_to(mod_all[:, B:B + 1], (L, B, 6, D))], axis=2)

    cs, sn = _rope_tables(S, CL)
    pairs, plan = _na_layout(S // GRID_W)
    pair_all = _na_bias_pairs(na_rpb, pairs)

    w_main_all = jnp.concatenate([w_in[:, :, :gate_col], w_in[:, :, gate_col + 2 * n_gate:]], axis=2).astype(BF16)
    ig = w_in[:, :, gate_col:gate_col + n_gate].reshape(L, D, 2, n_ml_heads)
    fg = w_in[:, :, gate_col + n_gate:gate_col + 2 * n_gate].reshape(L, D, 2, n_ml_heads)
    wg_all = jnp.pad(jnp.concatenate([ig, fg], axis=3), ((0, 0), (0, 0), (0, 0), (0, LANES - n_gate)))
    wg_all = wg_all.reshape(L, D, 2 * LANES).astype(BF16)
    gb = jnp.concatenate([ml_igate_b, ml_fgate_b], axis=2)
    gbc_all = jnp.pad(gb, ((0, 0), (0, 0), (0, LANES - n_gate)))[:, :, None, :]
    gbr_all = jnp.broadcast_to(jnp.pad(gb, ((0, 0), (0, 0), (0, 8 - n_gate)))[..., None], (L, 2, 8, LANES))
    qs = jnp.concatenate([jnp.full((1, qk_w // 2), ML_QK_DIM ** -0.5, F32), jnp.ones((1, qk_w // 2), F32)], axis=1)
    w_na_all, w_ml_all, w_out_all = w_na_proj.astype(BF16), w_ml_proj.astype(BF16), w_out.astype(BF16)
    w_up_all, w_down_all = w_up.astype(BF16), w_down.astype(BF16)
    vec = lambda a: a[:, None, :]

    xc, h = _prenorm(x, ctx, mods_all, vec(norm_mix_pre))
    for l in range(L):
        z, gcol, grow = _in_projection(h, cs, sn, w_main_all, wg_all, ml_conv_w, vec(ml_conv_b), qs, l,
                                       n_lat=S, tm=tm, tn=tn, na_w=na_w, qk_col=qk_col, v_col=v_col, mo_col=mo_col)
        na_o = _neighbourhood_attention(z, pair_all, plan, l, n_lat=S, n_heads=n_na_heads, na_w=na_w)
        ml_h = _mlstm(z, gcol, grow, gbc_all, gbr_all, l, n_heads=n_ml_heads, qk_col=qk_col, v_col=v_col)
        xc, h = _merge(na_o, ml_h, z, xc, mods_all, vec(ml_norm_g), vec(norm_mix_post), vec(norm_ffn_pre),
                       w_na_all, w_ml_all, w_out_all, l,
                       n_heads=n_ml_heads, n_lat=S, mo_col=mo_col, gn_col=gn_col, gm_col=gm_col)
        act = _ffn_up(h, w_up_all, ffn_conv_w, vec(ffn_conv_b), l, n_lat=S, tm=tm, tn=tn_ff)
        if l == L - 1:
            return _ffn_down(act, xc, mods_all, vec(norm_ffn_post), w_down_all, None, l, last=True, n_lat=S)
        xc, h = _ffn_down(act, xc, mods_all, vec(norm_ffn_post), w_down_all, vec(norm_mix_pre), l, last=False, n_lat=S)
```
